```python
import jax, jax.numpy as jnp
from jax import lax
import numpy as np

D_MODEL = 1024
BATCH = 32
SEQ = 256
DEPTH = 4
DEC_BATCH = 8
DEC_SEQ = 2048
PAST_LEN = 512

GRID_W = 64
N_MIXERS = 4
EPS = 1e-6
Q_BLOCK = 128
ROPE_THETA = 10000.0
POOL_WINDOWS = (2, 4, 8, 16)
POOL_GROUPS = 4
POOL_GW = D_MODEL // POOL_GROUPS
MLA_HEADS = 8
MLA_Q_LORA = 3 * D_MODEL // 8
MLA_KV_LORA = D_MODEL // 4
MLA_NOPE = 128
MLA_ROPE = 64
MLA_V = 128
MLA_SCALE = (MLA_NOPE + MLA_ROPE) ** -0.5
CONV_WIDTH = 31
GQA_HEADS = 8
GQA_KV_HEADS = 2
GQA_HEAD_DIM = 128
GQA_SCALE = GQA_HEAD_DIM ** -0.5
N_EXPERTS = 16
EXPERT_FF = D_MODEL // 2
EC_CAPACITY = 2
N_POOL = len(range(0, DEPTH, N_MIXERS))
N_MLA = len(range(1, DEPTH, N_MIXERS))
N_CONV = len(range(2, DEPTH, N_MIXERS))
N_GQA = len(range(3, DEPTH, N_MIXERS))

kernel_name = 'hybrid_diffusion_ctx_prefix_step'


def _rms(x, g):
    xf = x.astype(jnp.float32)
    y = xf * lax.rsqrt(jnp.mean(xf * xf, axis=-1, keepdims=True) + EPS)
    return (y * g.astype(jnp.float32)).astype(x.dtype)


def _modulate(x, g, shift, scale):
    return _rms(x, g) * (1 + scale[:, None, :]) + shift[:, None, :]


def _adaln(cond, w, b):
    return jnp.split(jax.nn.silu(cond) @ w + b, 6, axis=-1)


def _axial_rope_tables(n_tokens, rot_dim):
    rows = n_tokens // GRID_W
    row = jnp.repeat(jnp.arange(rows), GRID_W).astype(jnp.float32)
    col = jnp.tile(jnp.arange(GRID_W), rows).astype(jnp.float32)
    axis_dim = rot_dim // 2
    inv = ROPE_THETA ** (-jnp.arange(0, axis_dim, 2, dtype=jnp.float32) / axis_dim)
    ang = jnp.concatenate([row[:, None] * inv, col[:, None] * inv], axis=-1)
    return jnp.cos(ang), jnp.sin(ang)


def _rope(x, cos, sin):
    xf = x.astype(jnp.float32)
    half = x.shape[-1] // 2
    x1, x2 = xf[..., :half], xf[..., half:]
    cb, sb = cos[None, :, None, :], sin[None, :, None, :]
    return jnp.concatenate([x1 * cb - x2 * sb, x1 * sb + x2 * cb], axis=-1).astype(x.dtype)


def _attention(q_ctx, k_ctx, v_ctx, scale, q_lat=None, k_lat=None, v_lat=None):
    B, S, Hkv, G, _ = q_ctx.shape
    nb = S // Q_BLOCK

    def to_blocks(q):
        return jnp.moveaxis(q.reshape(B, nb, Q_BLOCK, *q.shape[2:]), 1, 0)

    if q_lat is None:
        qs, keys, v = (to_blocks(q_ctx),), (k_ctx,), v_ctx
    else:
        qs, keys = (to_blocks(q_ctx), to_blocks(q_lat)), (k_ctx, k_lat)
        v = jnp.concatenate([v_ctx, v_lat], axis=1)

    def block(qb):
        s = jnp.concatenate(
            [jnp.einsum('bqhgd,bkhd->bhgqk', qi, ki).astype(jnp.float32) for qi, ki in zip(qb, keys)],
            axis=-1) * scale
        p = jax.nn.softmax(s, axis=-1).astype(v.dtype)
        return jnp.einsum('bhgqk,bkhd->bqhgd', p, v)

    o = lax.map(block, qs)
    return jnp.moveaxis(o, 0, 1).reshape(B, S, -1)


def _pool_mixer(h, w, scale):
    B, S, _ = h.shape
    hf = h.astype(jnp.float32)
    cs = jnp.concatenate([jnp.zeros((B, 1, D_MODEL), jnp.float32), jnp.cumsum(hf, axis=1)], axis=1)
    t = jnp.arange(S)
    groups = []
    for gi, win in enumerate(POOL_WINDOWS):
        lo = jnp.clip(t - win // 2, 0, S)
        hi = jnp.clip(t + win // 2, 0, S)
        sl = slice(gi * POOL_GW, (gi + 1) * POOL_GW)
        csg = cs[:, :, sl]
        mean = (csg[:, hi] - csg[:, lo]) / (hi - lo).astype(jnp.float32)[None, :, None]
        groups.append(mean - hf[:, :, sl])
    pooled = jnp.stack(groups, axis=2).astype(h.dtype)
    y = jnp.einsum('bsgc,gcd->bsgd', pooled, w).reshape(B, S, D_MODEL)
    return y * scale


def _mla_project(h, w_down, g_q, w_uq, g_kv):
    B, S, _ = h.shape
    d = h @ w_down
    cq, ckv, krope = jnp.split(d, [MLA_Q_LORA, MLA_Q_LORA + MLA_KV_LORA], axis=-1)
    q = (_rms(cq, g_q) @ w_uq).reshape(B, S, MLA_HEADS, MLA_NOPE + MLA_ROPE)
    return q[..., :MLA_NOPE], q[..., MLA_NOPE:], _rms(ckv, g_kv), krope


def _mla_expand(ckv, krope, w_ukv):
    B, L, _ = ckv.shape
    kv = (ckv @ w_ukv).reshape(B, L, MLA_HEADS, MLA_NOPE + MLA_V)
    k = jnp.concatenate(
        [kv[..., :MLA_NOPE], jnp.broadcast_to(krope[:, :, None, :], (B, L, MLA_HEADS, MLA_ROPE))], axis=-1)
    return k, kv[..., MLA_NOPE:]


def _mla_context(h, w_down, g_q, w_uq, g_kv, w_ukv, w_o):
    qn, qr, ckv, kr = _mla_project(h, w_down, g_q, w_uq, g_kv)
    k, v = _mla_expand(ckv, kr, w_ukv)
    q = jnp.concatenate([qn, qr], axis=-1)[:, :, :, None, :]
    return _attention(q, k, v, MLA_SCALE) @ w_o, ckv, kr


def _mla_latent(h, ctx_ckv, ctx_kr, w_down, g_q, w_uq, g_kv, w_ukv, w_o):
    qn, qr, ckv, kr = _mla_project(h, w_down, g_q, w_uq, g_kv)
    cos, sin = _axial_rope_tables(h.shape[1], MLA_ROPE)
    q_lat = jnp.concatenate([qn, _rope(qr, cos, sin)], axis=-1)[:, :, :, None, :]
    q_ctx = jnp.concatenate([qn, qr], axis=-1)[:, :, :, None, :]
    k_lat, v_lat = _mla_expand(ckv, _rope(kr[:, :, None, :], cos, sin)[:, :, 0], w_ukv)
    k_ctx, v_ctx = _mla_expand(ctx_ckv, ctx_kr, w_ukv)
    return _attention(q_ctx, k_ctx, v_ctx, MLA_SCALE, q_lat, k_lat, v_lat) @ w_o


def _conv_module(h, w_pw1, b_pw1, w_dw, b_dw, ln_g, ln_b, w_pw2, b_pw2):
    u = h @ w_pw1 + b_pw1
    u = u[..., :D_MODEL] * jax.nn.sigmoid(u[..., D_MODEL:])
    u = lax.conv_general_dilated(
        u, w_dw[:, None, :], window_strides=(1,),
        padding=[(CONV_WIDTH // 2, CONV_WIDTH // 2)],
        dimension_numbers=('NWC', 'WIO', 'NWC'), feature_group_count=D_MODEL) + b_dw
    uf = u.astype(jnp.float32)
    mu = jnp.mean(uf, axis=-1, keepdims=True)
    var = jnp.mean(jnp.square(uf - mu), axis=-1, keepdims=True)
    u = ((uf - mu) * lax.rsqrt(var + EPS) * ln_g + ln_b).astype(h.dtype)
    return jax.nn.silu(u) @ w_pw2 + b_pw2


def _gqa_project(h, w_qkv, g_q, g_k):
    B, S, _ = h.shape
    qkv = (h @ w_qkv).reshape(B, S, GQA_HEADS + 2 * GQA_KV_HEADS, GQA_HEAD_DIM)
    q, k, v = jnp.split(qkv, [GQA_HEADS, GQA_HEADS + GQA_KV_HEADS], axis=2)
    return _rms(q, g_q), _rms(k, g_k), v


def _group(q):
    B, S, H, d = q.shape
    return q.reshape(B, S, GQA_KV_HEADS, H // GQA_KV_HEADS, d)


def _gqa_context(h, w_qkv, g_q, g_k, w_o):
    q, k, v = _gqa_project(h, w_qkv, g_q, g_k)
    return _attention(_group(q), k, v, GQA_SCALE) @ w_o, k, v


def _gqa_latent(h, ctx_k, ctx_v, w_qkv, g_q, g_k, w_o):
    q, k, v = _gqa_project(h, w_qkv, g_q, g_k)
    cos, sin = _axial_rope_tables(h.shape[1], GQA_HEAD_DIM)
    o = _attention(_group(q), ctx_k, ctx_v, GQA_SCALE, _group(_rope(q, cos, sin)), _rope(k, cos, sin), v)
    return o @ w_o


def _expert_choice(h, router_w, w_gate, w_up, w_down):
    B, S, _ = h.shape
    cap = EC_CAPACITY * S // N_EXPERTS
    aff = jax.nn.softmax((h @ router_w).astype(jnp.float32), axis=-1)
    gates, idx = lax.top_k(jnp.swapaxes(aff, 1, 2), cap)
    xg = jax.vmap(lambda xb, ib: xb[ib])(h, idx)
    hid = jax.nn.silu(jnp.einsum('becd,edf->becf', xg, w_gate)) * jnp.einsum('becd,edf->becf', xg, w_up)
    yo = jnp.einsum('becf,efd->becd', hid, w_down) * gates[..., None].astype(h.dtype)
    return jax.vmap(lambda ib, vb: jnp.zeros((S, D_MODEL), vb.dtype).at[ib.reshape(-1)].add(
        vb.reshape(-1, D_MODEL)))(idx, yo)


def setup_inputs(seed: int = 0) -> dict:
    key = jax.random.key(seed)
    ks = iter(jax.random.split(key, 40))
    D = D_MODEL

    def nrm(shape, scale=1.0):
        return jax.random.normal(next(ks), shape, jnp.float32) * scale

    def gain(shape):
        return 1.0 + 0.05 * nrm(shape)

    return {
        'x_prompt': nrm((BATCH, SEQ, D)),
        'x_sample': nrm((DEC_BATCH, DEC_SEQ, D)),
        'c': nrm((DEC_BATCH, D)),
        'cache_mla_ckv': nrm((DEC_BATCH, N_MLA, PAST_LEN, MLA_KV_LORA)),
        'cache_mla_krope': nrm((DEC_BATCH, N_MLA, PAST_LEN, MLA_ROPE)),
        'cache_gqa_k': nrm((DEC_BATCH, N_GQA, PAST_LEN, GQA_KV_HEADS, GQA_HEAD_DIM)),
        'cache_gqa_v': nrm((DEC_BATCH, N_GQA, PAST_LEN, GQA_KV_HEADS, GQA_HEAD_DIM)),
        'c_ctx': nrm((D,)),
        'ada_w': nrm((DEPTH, D, 6 * D), 0.3 * D ** -0.5),
        'ada_b': nrm((DEPTH, 6 * D), 0.02),
        'norm1_g': gain((DEPTH, D)),
        'norm2_g': gain((DEPTH, D)),
        'final_g': gain((D,)),
        'pool_w': nrm((N_POOL, POOL_GROUPS, POOL_GW, POOL_GW), POOL_GW ** -0.5),
        'pool_scale': gain((N_POOL, D)),
        'mla_w_down': nrm((N_MLA, D, MLA_Q_LORA + MLA_KV_LORA + MLA_ROPE), D ** -0.5),
        'mla_g_q': gain((N_MLA, MLA_Q_LORA)),
        'mla_w_uq': nrm((N_MLA, MLA_Q_LORA, MLA_HEADS * (MLA_NOPE + MLA_ROPE)), MLA_Q_LORA ** -0.5),
        'mla_g_kv': gain((N_MLA, MLA_KV_LORA)),
        'mla_w_ukv': nrm((N_MLA, MLA_KV_LORA, MLA_HEADS * (MLA_NOPE + MLA_V)), MLA_KV_LORA ** -0.5),
        'mla_w_o': nrm((N_MLA, MLA_HEADS * MLA_V, D), (MLA_HEADS * MLA_V) ** -0.5),
        'conv_w_pw1': nrm((N_CONV, D, 2 * D), D ** -0.5),
        'conv_b_pw1': nrm((N_CONV, 2 * D), 0.02),
        'conv_w_dw': nrm((N_CONV, CONV_WIDTH, D), CONV_WIDTH ** -0.5),
        'conv_b_dw': nrm((N_CONV, D), 0.02),
        'conv_ln_g': gain((N_CONV, D)),
        'conv_ln_b': nrm((N_CONV, D), 0.02),
        'conv_w_pw2': nrm((N_CONV, D, D), D ** -0.5),
        'conv_b_pw2': nrm((N_CONV, D), 0.02),
        'gqa_w_qkv': nrm((N_GQA, D, (GQA_HEADS + 2 * GQA_KV_HEADS) * GQA_HEAD_DIM), D ** -0.5),
        'gqa_g_q': gain((N_GQA, GQA_HEAD_DIM)),
        'gqa_g_k': gain((N_GQA, GQA_HEAD_DIM)),
        'gqa_w_o': nrm((N_GQA, GQA_HEADS * GQA_HEAD_DIM, D), (GQA_HEADS * GQA_HEAD_DIM) ** -0.5),
        'router_w': nrm((DEPTH, D, N_EXPERTS), D ** -0.5),
        'moe_w_gate': nrm((DEPTH, N_EXPERTS, D, EXPERT_FF), D ** -0.5),
        'moe_w_up': nrm((DEPTH, N_EXPERTS, D, EXPERT_FF), D ** -0.5),
        'moe_w_down': nrm((DEPTH, N_EXPERTS, EXPERT_FF, D), EXPERT_FF ** -0.5),
    }


def reference(x_prompt, x_sample, c, cache_mla_ckv, cache_mla_krope, cache_gqa_k, cache_gqa_v, c_ctx,
              ada_w, ada_b, norm1_g, norm2_g, final_g, pool_w, pool_scale,
              mla_w_down, mla_g_q, mla_w_uq, mla_g_kv, mla_w_ukv, mla_w_o,
              conv_w_pw1, conv_b_pw1, conv_w_dw, conv_b_dw, conv_ln_g, conv_ln_b, conv_w_pw2, conv_b_pw2,
              gqa_w_qkv, gqa_g_q, gqa_g_k, gqa_w_o,
              router_w, moe_w_gate, moe_w_up, moe_w_down):
    xc, xl = x_prompt, x_sample
    new_ckv, new_kr, new_k, new_v = [], [], [], []
    for l in range(DEPTH):
        kind, j = l % N_MIXERS, l // N_MIXERS
        mc = _adaln(c_ctx[None, :], ada_w[l], ada_b[l])
        ml = _adaln(c, ada_w[l], ada_b[l])
        hc = _modulate(xc, norm1_g[l], mc[0], mc[1])
        hl = _modulate(xl, norm1_g[l], ml[0], ml[1])
        if kind == 0:
            oc = _pool_mixer(hc, pool_w[j], pool_scale[j])
            ol = _pool_mixer(hl, pool_w[j], pool_scale[j])
        elif kind == 1:
            mla_p = (mla_w_down[j], mla_g_q[j], mla_w_uq[j], mla_g_kv[j], mla_w_ukv[j], mla_w_o[j])
            oc, ckv, kr = _mla_context(hc, *mla_p)
            ol = _mla_latent(hl, cache_mla_ckv[:, j], cache_mla_krope[:, j], *mla_p)
            new_ckv.append(ckv)
            new_kr.append(kr)
        elif kind == 2:
            conv_p = (conv_w_pw1[j], conv_b_pw1[j], conv_w_dw[j], conv_b_dw[j],
                      conv_ln_g[j], conv_ln_b[j], conv_w_pw2[j], conv_b_pw2[j])
            oc = _conv_module(hc, *conv_p)
            ol = _conv_module(hl, *conv_p)
        else:
            gqa_p = (gqa_w_qkv[j], gqa_g_q[j], gqa_g_k[j], gqa_w_o[j])
            oc, k, v = _gqa_context(hc, *gqa_p)
            ol = _gqa_latent(hl, cache_gqa_k[:, j], cache_gqa_v[:, j], *gqa_p)
            new_k.append(k)
            new_v.append(v)
        xc = xc + mc[2][:, None, :] * oc
        xl = xl + ml[2][:, None, :] * ol
        moe_p = (router_w[l], moe_w_gate[l], moe_w_up[l], moe_w_down[l])
        xc = xc + mc[5][:, None, :] * _expert_choice(_modulate(xc, norm2_g[l], mc[3], mc[4]), *moe_p)
        xl = xl + ml[5][:, None, :] * _expert_choice(_modulate(xl, norm2_g[l], ml[3], ml[4]), *moe_p)
    y_prompt = _rms(xc, final_g)
    y_sample = _rms(xl, final_g)
    state_mla_ckv = jnp.stack(new_ckv, axis=1)
    state_mla_krope = jnp.stack(new_kr, axis=1)
    state_gqa_k = jnp.stack(new_k, axis=1)
    state_gqa_v = jnp.stack(new_v, axis=1)
    return (y_prompt, y_sample, state_mla_ckv, state_mla_krope, state_gqa_k, state_gqa_v)
```

```python
import functools

import jax
import jax.numpy as jnp
from jax import lax
from jax.experimental import pallas as pl
from jax.experimental.pallas import tpu as pltpu

D = 1024
DEPTH = 4
GRID_W = 64
EPS = 1e-6
ROPE_THETA = 10000.0
POOL_WINDOWS = (2, 4, 8, 16)
POOL_GW = 256
MLA_HEADS = 8
MLA_Q_LORA = 384
MLA_KV_LORA = 256
MLA_NOPE = 128
MLA_ROPE = 64
MLA_V = 128
MLA_SCALE = (MLA_NOPE + MLA_ROPE) ** -0.5
CONV_WIDTH = 31
GQA_HEADS = 8
GQA_KV_HEADS = 2
GQA_HEAD_DIM = 128
GQA_SCALE = GQA_HEAD_DIM ** -0.5
N_EXPERTS = 16
EXPERT_FF = 512
EC_CAPACITY = 2


def _ffn_kernel(x_ref, wg_ref, wu_ref, wd_ref, g_ref, o_ref):
    x = x_ref[0]
    a = jnp.dot(x, wg_ref[0], preferred_element_type=jnp.float32)
    b = jnp.dot(x, wu_ref[0], preferred_element_type=jnp.float32)
    hid = (a * jax.nn.sigmoid(a) * b).astype(jnp.bfloat16)
    y = jnp.dot(hid, wd_ref[0], preferred_element_type=jnp.float32)
    o_ref[0] = y * g_ref[0]


def _expert_ffn(xg, wg, wu, wd, gates):
    E, R, _ = xg.shape
    tr = 512
    return pl.pallas_call(
        _ffn_kernel,
        out_shape=jax.ShapeDtypeStruct((E, R, D), jnp.float32),
        grid=(E, R // tr),
        in_specs=[
            pl.BlockSpec((1, tr, D), lambda e, r: (e, r, 0)),
            pl.BlockSpec((1, D, EXPERT_FF), lambda e, r: (e, 0, 0)),
            pl.BlockSpec((1, D, EXPERT_FF), lambda e, r: (e, 0, 0)),
            pl.BlockSpec((1, EXPERT_FF, D), lambda e, r: (e, 0, 0)),
            pl.BlockSpec((1, tr, 1), lambda e, r: (e, r, 0)),
        ],
        out_specs=pl.BlockSpec((1, tr, D), lambda e, r: (e, r, 0)),
        name="expert_ffn",
    )(xg, wg, wu, wd, gates)


def _moe(h, router_w, wg, wu, wd):
    B, S, _ = h.shape
    cap = EC_CAPACITY * S // N_EXPERTS
    aff = jax.nn.softmax((h @ router_w).astype(jnp.float32), axis=-1)
    gates, idx = lax.top_k(jnp.swapaxes(aff, 1, 2), cap)
    xg = jax.vmap(lambda xb, ib: xb[ib])(h, idx)
    xg = jnp.swapaxes(xg, 0, 1).reshape(N_EXPERTS, B * cap, D).astype(jnp.bfloat16)
    gt = jnp.swapaxes(gates, 0, 1).reshape(N_EXPERTS, B * cap, 1)
    yo = _expert_ffn(xg, wg, wu, wd, gt)
    yo = jnp.swapaxes(yo.reshape(N_EXPERTS, B, cap, D), 0, 1)
    return jax.vmap(lambda ib, vb: jnp.zeros((S, D), vb.dtype).at[ib.reshape(-1)].add(
        vb.reshape(-1, D)))(idx, yo)


def _rms(x, g):
    y = x * lax.rsqrt(jnp.mean(x * x, axis=-1, keepdims=True) + EPS)
    return y * g


def _modulate(x, g, shift, scale):
    return _rms(x, g) * (1 + scale[:, None, :]) + shift[:, None, :]


def _rope_tables(n_tokens, rot_dim):
    rows = n_tokens // GRID_W
    row = jnp.repeat(jnp.arange(rows), GRID_W).astype(jnp.float32)
    col = jnp.tile(jnp.arange(GRID_W), rows).astype(jnp.float32)
    axis_dim = rot_dim // 2
    inv = ROPE_THETA ** (-jnp.arange(0, axis_dim, 2, dtype=jnp.float32) / axis_dim)
    ang = jnp.concatenate([row[:, None] * inv, col[:, None] * inv], axis=-1)
    return jnp.cos(ang), jnp.sin(ang)


def _rope(x, cos, sin):
    half = x.shape[-1] // 2
    x1, x2 = x[..., :half], x[..., half:]
    cb, sb = cos[None, :, None, :], sin[None, :, None, :]
    return jnp.concatenate([x1 * cb - x2 * sb, x1 * sb + x2 * cb], axis=-1)


def _attn(q_ctx, k_ctx, v_ctx, scale, q_lat=None, k_lat=None, v_lat=None):
    B, S, Hkv, G, _ = q_ctx.shape
    s = jnp.einsum('bqhgd,bkhd->bhgqk', q_ctx, k_ctx)
    v = v_ctx
    if q_lat is not None:
        s = jnp.concatenate([s, jnp.einsum('bqhgd,bkhd->bhgqk', q_lat, k_lat)], axis=-1)
        v = jnp.concatenate([v_ctx, v_lat], axis=1)
    p = jax.nn.softmax(s * scale, axis=-1)
    return jnp.einsum('bhgqk,bkhd->bqhgd', p, v).reshape(B, S, -1)


def _pool(h, w, scale):
    B, S, _ = h.shape
    cs = jnp.concatenate([jnp.zeros((B, 1, D), jnp.float32), jnp.cumsum(h, axis=1)], axis=1)
    t = jnp.arange(S)
    groups = []
    for gi, win in enumerate(POOL_WINDOWS):
        lo = jnp.clip(t - win // 2, 0, S)
        hi = jnp.clip(t + win // 2, 0, S)
        sl = slice(gi * POOL_GW, (gi + 1) * POOL_GW)
        csg = cs[:, :, sl]
        mean = (csg[:, hi] - csg[:, lo]) / (hi - lo).astype(jnp.float32)[None, :, None]
        groups.append(mean - h[:, :, sl])
    pooled = jnp.stack(groups, axis=2)
    y = jnp.einsum('bsgc,gcd->bsgd', pooled, w).reshape(B, S, D)
    return y * scale


def _mla_project(h, w_down, g_q, w_uq, g_kv):
    B, S, _ = h.shape
    d = h @ w_down
    cq, ckv, krope = jnp.split(d, [MLA_Q_LORA, MLA_Q_LORA + MLA_KV_LORA], axis=-1)
    q = (_rms(cq, g_q) @ w_uq).reshape(B, S, MLA_HEADS, MLA_NOPE + MLA_ROPE)
    return q[..., :MLA_NOPE], q[..., MLA_NOPE:], _rms(ckv, g_kv), krope


def _mla_expand(ckv, krope, w_ukv):
    B, L, _ = ckv.shape
    kv = (ckv @ w_ukv).reshape(B, L, MLA_HEADS, MLA_NOPE + MLA_V)
    k = jnp.concatenate(
        [kv[..., :MLA_NOPE], jnp.broadcast_to(krope[:, :, None, :], (B, L, MLA_HEADS, MLA_ROPE))], axis=-1)
    return k, kv[..., MLA_NOPE:]


def _mla_context(h, w_down, g_q, w_uq, g_kv, w_ukv, w_o):
    qn, qr, ckv, kr = _mla_project(h, w_down, g_q, w_uq, g_kv)
    k, v = _mla_expand(ckv, kr, w_ukv)
    q = jnp.concatenate([qn, qr], axis=-1)[:, :, :, None, :]
    return _attn(q, k, v, MLA_SCALE) @ w_o, ckv, kr


def _mla_latent(h, ctx_ckv, ctx_kr, w_down, g_q, w_uq, g_kv, w_ukv, w_o):
    qn, qr, ckv, kr = _mla_project(h, w_down, g_q, w_uq, g_kv)
    cos, sin = _rope_tables(h.shape[1], MLA_ROPE)
    q_lat = jnp.concatenate([qn, _rope(qr, cos, sin)], axis=-1)[:, :, :, None, :]
    q_ctx = jnp.concatenate([qn, qr], axis=-1)[:, :, :, None, :]
    k_lat, v_lat = _mla_expand(ckv, _rope(kr[:, :, None, :], cos, sin)[:, :, 0], w_ukv)
    k_ctx, v_ctx = _mla_expand(ctx_ckv, ctx_kr, w_ukv)
    return _attn(q_ctx, k_ctx, v_ctx, MLA_SCALE, q_lat, k_lat, v_lat) @ w_o


def _conv_module(h, w_pw1, b_pw1, w_dw, b_dw, ln_g, ln_b, w_pw2, b_pw2):
    u = h @ w_pw1 + b_pw1
    u = u[..., :D] * jax.nn.sigmoid(u[..., D:])
    u = lax.conv_general_dilated(
        u, w_dw[:, None, :], window_strides=(1,),
        padding=[(CONV_WIDTH // 2, CONV_WIDTH // 2)],
        dimension_numbers=('NWC', 'WIO', 'NWC'), feature_group_count=D) + b_dw
    mu = jnp.mean(u, axis=-1, keepdims=True)
    var = jnp.mean(jnp.square(u - mu), axis=-1, keepdims=True)
    u = (u - mu) * lax.rsqrt(var + EPS) * ln_g + ln_b
    return jax.nn.silu(u) @ w_pw2 + b_pw2


def _gqa_project(h, w_qkv, g_q, g_k):
    B, S, _ = h.shape
    qkv = (h @ w_qkv).reshape(B, S, GQA_HEADS + 2 * GQA_KV_HEADS, GQA_HEAD_DIM)
    q, k, v = jnp.split(qkv, [GQA_HEADS, GQA_HEADS + GQA_KV_HEADS], axis=2)
    return _rms(q, g_q), _rms(k, g_k), v


def _group(q):
    B, S, H, d = q.shape
    return q.reshape(B, S, GQA_KV_HEADS, H // GQA_KV_HEADS, d)


def _gqa_context(h, w_qkv, g_q, g_k, w_o):
    q, k, v = _gqa_project(h, w_qkv, g_q, g_k)
    return _attn(_group(q), k, v, GQA_SCALE) @ w_o, k, v


def _gqa_latent(h, ctx_k, ctx_v, w_qkv, g_q, g_k, w_o):
    q, k, v = _gqa_project(h, w_qkv, g_q, g_k)
    cos, sin = _rope_tables(h.shape[1], GQA_HEAD_DIM)
    o = _attn(_group(q), ctx_k, ctx_v, GQA_SCALE, _group(_rope(q, cos, sin)), _rope(k, cos, sin), v)
    return o @ w_o


def kernel(x_prompt, x_sample, c, cache_mla_ckv, cache_mla_krope, cache_gqa_k, cache_gqa_v, c_ctx, ada_w, ada_b, norm1_g, norm2_g, final_g, pool_w, pool_scale, mla_w_down, mla_g_q, mla_w_uq, mla_g_kv, mla_w_ukv, mla_w_o, conv_w_pw1, conv_b_pw1, conv_w_dw, conv_b_dw, conv_ln_g, conv_ln_b, conv_w_pw2, conv_b_pw2, gqa_w_qkv, gqa_g_q, gqa_g_k, gqa_w_o, router_w, moe_w_gate, moe_w_up, moe_w_down):
    xc, xl = x_prompt, x_sample
    bf = jnp.bfloat16
    wg_all, wu_all, wd_all = moe_w_gate.astype(bf), moe_w_up.astype(bf), moe_w_down.astype(bf)
    outs = {}
    for l in range(DEPTH):
        mc = jnp.split(jax.nn.silu(c_ctx[None, :]) @ ada_w[l] + ada_b[l], 6, axis=-1)
        ml = jnp.split(jax.nn.silu(c) @ ada_w[l] + ada_b[l], 6, axis=-1)
        hc = _modulate(xc, norm1_g[l], mc[0], mc[1])
        hl = _modulate(xl, norm1_g[l], ml[0], ml[1])
        if l == 0:
            oc = _pool(hc, pool_w[0], pool_scale[0])
            ol = _pool(hl, pool_w[0], pool_scale[0])
        elif l == 1:
            p = (mla_w_down[0], mla_g_q[0], mla_w_uq[0], mla_g_kv[0], mla_w_ukv[0], mla_w_o[0])
            oc, ckv, kr = _mla_context(hc, *p)
            ol = _mla_latent(hl, cache_mla_ckv[:, 0], cache_mla_krope[:, 0], *p)
            outs['ckv'], outs['kr'] = ckv[:, None], kr[:, None]
        elif l == 2:
            p = (conv_w_pw1[0], conv_b_pw1[0], conv_w_dw[0], conv_b_dw[0],
                 conv_ln_g[0], conv_ln_b[0], conv_w_pw2[0], conv_b_pw2[0])
            oc = _conv_module(hc, *p)
            ol = _conv_module(hl, *p)
        else:
            p = (gqa_w_qkv[0], gqa_g_q[0], gqa_g_k[0], gqa_w_o[0])
            oc, k, v = _gqa_context(hc, *p)
            ol = _gqa_latent(hl, cache_gqa_k[:, 0], cache_gqa_v[:, 0], *p)
            outs['k'], outs['v'] = k[:, None], v[:, None]
        xc = xc + mc[2][:, None, :] * oc
        xl = xl + ml[2][:, None, :] * ol
        mp = (router_w[l], wg_all[l], wu_all[l], wd_all[l])
        xc = xc + mc[5][:, None, :] * _moe(_modulate(xc, norm2_g[l], mc[3], mc[4]), *mp)
        xl = xl + ml[5][:, None, :] * _moe(_modulate(xl, norm2_g[l], ml[3], ml[4]), *mp)
    return (_rms(xc, final_g), _rms(xl, final_g), outs['ckv'], outs['kr'], outs['k'], outs['v'])
```

```python
import functools
import math

import jax
import jax.numpy as jnp
from jax import lax
from jax.experimental import pallas as pl
from jax.experimental.pallas import tpu as pltpu

F32 = jnp.float32
BF16 = jnp.bfloat16
I32 = jnp.int32

D = 1024
DEPTH = 4
EPS = 1e-6
GRID_W = 64
ROPE_THETA = 10000.0
LOG2E = math.log2(math.e)

TB = 256
TBW = 512
TBP = 256
HALO = 16
N_CTX_SEQ, CTX_LEN = 32, 256
N_LAT_SEQ, LAT_LEN = 8, 2048
PAST_LEN = 512
N_CTX_TOK = N_CTX_SEQ * CTX_LEN
N_LAT_TOK = N_LAT_SEQ * LAT_LEN
N_TOK = N_CTX_TOK + N_LAT_TOK
N_CTX_BLK = N_CTX_TOK // TB
LAT_BLK = LAT_LEN // TB
N_BLK = N_TOK // TB
N_COND = 16

POOL_WINDOWS = (2, 4, 8, 16)
POOL_GW = 256
MLA_HEADS = 8
MLA_Q_LORA = 384
MLA_KV_LORA = 256
MLA_NOPE = 128
MLA_ROPE = 64
MLA_DOWN_PAD = 768
MLA_QK = 256
MLA_SCALE = (MLA_NOPE + MLA_ROPE) ** -0.5
CONV_WIDTH = 31
GQA_HEADS = 8
GQA_KV_HEADS = 2
GQA_HEAD_DIM = 128
GQA_SCALE = GQA_HEAD_DIM ** -0.5
N_EXPERTS = 16
EXPERT_FF = 512

UNIT = 2048
UNIT_BLK = UNIT // TB
GRP_BLK = 16
N_GRP = N_BLK // GRP_BLK
GRP_ROWS = 512
WIN = 64
XY_ROWS = GRP_ROWS + WIN
MOE_BPS = 2
MOE_TS = GRP_BLK // MOE_BPS

NT = (((1,), (1,)), ((), ()))
TN = (((0,), (0,)), ((), ()))

VMEM_LIMIT = 56 * 1024 * 1024


def _cparams(n_axes):
    return pltpu.CompilerParams(dimension_semantics=("arbitrary",) * n_axes,
                                vmem_limit_bytes=VMEM_LIMIT)


def _cond_row(b, tb=TB):
    n_ctx = N_CTX_TOK // tb
    return jnp.where(b < n_ctx, 0, 1 + (b - n_ctx) // (LAT_LEN // tb))


def _lat_blk(b, tb=TB):
    n_ctx = N_CTX_TOK // tb
    return jnp.where(b < n_ctx, 0, (b - n_ctx) % (LAT_LEN // tb))


def _mods_spec(layer, blk_of=lambda *g: g[0], tb=TB):
    return pl.BlockSpec((1, 1, 6 * D),
                        lambda *g: (layer * N_COND + _cond_row(blk_of(*g), tb), 0, 0))


def _full(shape):
    return pl.BlockSpec(shape, lambda *g: (0,) * len(shape))


def _tok(width, blk_of=lambda *g: g[0], tb=TB):
    return pl.BlockSpec((tb, width), lambda *g: (blk_of(*g), 0))


def _ctx_only(width, tb=TB):
    return pl.BlockSpec((tb, width), lambda b: (jnp.minimum(b, N_CTX_TOK // tb - 1), 0))


def _lat_only(width, tb=TB):
    return pl.BlockSpec((tb, width), lambda b: (jnp.maximum(b - N_CTX_TOK // tb, 0), 0))


def _halo_specs(width):
    per = TB // HALO
    prev = pl.BlockSpec((HALO, width), lambda b: (jnp.maximum(b * per - 1, 0), 0))
    nxt = pl.BlockSpec((HALO, width), lambda b: (jnp.minimum((b + 1) * per, N_TOK // HALO - 1), 0))
    return prev, nxt


def _seq_edges(b):
    lat = b >= N_CTX_BLK
    j = _lat_blk(b)
    return lat & (j > 0), lat & (j < LAT_BLK - 1)


def _rms_rows(x):
    return x * lax.rsqrt(jnp.mean(x * x, axis=-1, keepdims=True) + EPS)


def _modulate(x, g, shift, scale):
    return _rms_rows(x) * g * (1.0 + scale) + shift


def _silu(x):
    return x * jax.nn.sigmoid(x)


def _bdot(a, b):
    return jnp.dot(a, b, preferred_element_type=F32)


def _adaln_kernel(c_ref, w_ref, b_ref, o_ref):
    s = _silu(c_ref[...]).astype(BF16)
    o_ref[0] = _bdot(s, w_ref[0].astype(BF16)) + b_ref[0]


def _adaln(cond, ada_w, ada_b):
    tn = 1536
    return pl.pallas_call(
        _adaln_kernel,
        out_shape=jax.ShapeDtypeStruct((DEPTH, N_COND, 6 * D), F32),
        grid=(DEPTH, 6 * D // tn),
        in_specs=[
            _full((N_COND, D)),
            pl.BlockSpec((1, D, tn), lambda l, n: (l, 0, n)),
            pl.BlockSpec((1, 1, tn), lambda l, n: (l, 0, n)),
        ],
        out_specs=pl.BlockSpec((1, N_COND, tn), lambda l, n: (l, 0, n)),
        compiler_params=_cparams(2),
        name="adaln",
    )(cond, ada_w, ada_b.reshape(DEPTH, 1, 6 * D))


def _post(x, y, m, n2g_ref, rwh_ref, rwl_ref, x1_ref, h2_ref, lg_ref):
    x1 = x + m[:, 2 * D:3 * D] * y
    x1_ref[...] = x1
    h2 = _modulate(x1, n2g_ref[...], m[:, 3 * D:4 * D], m[:, 4 * D:5 * D])
    hh = h2.astype(BF16)
    hl = (h2 - hh.astype(F32)).astype(BF16)
    h2_ref[...] = hh
    rwh, rwl = rwh_ref[...], rwl_ref[...]
    lg = lax.dot_general(rwh, hh, NT, preferred_element_type=F32)
    lg += lax.dot_general(rwl, hh, NT, preferred_element_type=F32)
    lg += lax.dot_general(rwh, hl, NT, preferred_element_type=F32)
    lg_ref[...] = lg


_POST_IN = [_full((1, D)), _full((N_EXPERTS, D)), _full((N_EXPERTS, D))]
_POST_OUT_SHAPES = [jax.ShapeDtypeStruct((N_TOK, D), F32),
                    jax.ShapeDtypeStruct((N_TOK, D), BF16),
                    jax.ShapeDtypeStruct((N_EXPERTS, N_TOK), F32)]


def _post_out_specs(tb=TB):
    return [_tok(D, tb=tb), _tok(D, tb=tb), pl.BlockSpec((N_EXPERTS, tb), lambda b: (0, b))]


def _pool_kernel(xc_ref, xl_ref, xp_ref, xn_ref, mods_ref, n1g_ref, pw_ref, ps_ref,
                 n2g_ref, rwh_ref, rwl_ref, x1_ref, h2_ref, lg_ref):
    b = pl.program_id(0)
    has_prev, has_next = _seq_edges(b)
    m = mods_ref[0]
    g, sh, sc = n1g_ref[...], m[:, 0:D], m[:, D:2 * D]
    x = jnp.where(b >= N_CTX_BLK, xl_ref[...], xc_ref[...])
    hc = _modulate(x, g, sh, sc)
    hext = jnp.concatenate([_modulate(xp_ref[...], g, sh, sc), hc,
                            _modulate(xn_ref[...], g, sh, sc)], axis=0)
    ext = TB + 2 * HALO
    t = lax.broadcasted_iota(I32, (TB, ext), 0)
    u = lax.broadcasted_iota(I32, (TB, ext), 1) - HALO
    valid = ((u >= 0) | has_prev) & ((u < TB) | has_next)
    tc = lax.broadcasted_iota(I32, (TB, 1), 0)
    lo_lim = jnp.where(has_prev, -HALO, 0)
    hi_lim = jnp.where(has_next, TB + HALO, TB)
    ys = []
    for gi, win in enumerate(POOL_WINDOWS):
        hw = win // 2
        band = jnp.where((u >= t - hw) & (u < t + hw) & valid, 1.0, 0.0).astype(BF16)
        cnt = (jnp.minimum(tc + hw, hi_lim) - jnp.maximum(tc - hw, lo_lim)).astype(F32)
        hg = hext[:, gi * POOL_GW:(gi + 1) * POOL_GW]
        hi = hg.astype(BF16)
        lo = (hg - hi.astype(F32)).astype(BF16)
        tot = _bdot(band, hi) + _bdot(band, lo)
        pooled = tot / cnt - hc[:, gi * POOL_GW:(gi + 1) * POOL_GW]
        ys.append(_bdot(pooled.astype(BF16), pw_ref[gi]))
    y = jnp.concatenate(ys, axis=1) * ps_ref[...]
    _post(x, y, m, n2g_ref, rwh_ref, rwl_ref, x1_ref, h2_ref, lg_ref)


def _pool_layer(x_ctx, x_lat, mods, n1g, pool_w, pool_scale, n2g, rwh, rwl):
    per = TB // HALO
    lat_b = lambda b: jnp.maximum(b - N_CTX_BLK, 0)
    prev = pl.BlockSpec((HALO, D), lambda b: (jnp.maximum(lat_b(b) * per - 1, 0), 0))
    nxt = pl.BlockSpec((HALO, D),
                       lambda b: (jnp.minimum((lat_b(b) + 1) * per, N_LAT_TOK // HALO - 1), 0))
    return pl.pallas_call(
        _pool_kernel,
        out_shape=_POST_OUT_SHAPES,
        grid=(N_BLK,),
        in_specs=[_ctx_only(D), _lat_only(D), prev, nxt, _mods_spec(0), _full((1, D)),
                  _full((4, POOL_GW, POOL_GW)), _full((1, D))] + _POST_IN,
        out_specs=_post_out_specs(),
        compiler_params=_cparams(1),
        name="pool_mixer",
    )(x_ctx, x_lat, x_lat, x_lat, mods, n1g, pool_w, pool_scale, n2g, rwh, rwl)


def _rope_half32(v, cos_t, sin_t):
    lane = lax.broadcasted_iota(I32, v.shape, 1)
    swapped = jnp.where(lane < 32, pltpu.roll(v, 96, axis=1), pltpu.roll(v, 32, axis=1))
    return v * cos_t + swapped * sin_t


def _mla_proj_kernel(x_ref, mods_ref, n1g_ref, wd_ref, gq_ref, wuq_ref, gkv_ref, wukv_ref,
                     cos_ref, sin_ref, qc_ref, ql_ref, k_ref, v_ref, ckv_ref, kr_ref):
    lat = pl.program_id(0) >= N_CTX_TOK // TBP
    m = mods_ref[0]
    h = _modulate(x_ref[...], n1g_ref[...], m[:, 0:D], m[:, D:2 * D])
    d = _bdot(h.astype(BF16), wd_ref[...])
    cq = d[:, :MLA_Q_LORA]
    ckv = d[:, MLA_Q_LORA:MLA_Q_LORA + MLA_KV_LORA]
    krp = d[:, MLA_Q_LORA + MLA_KV_LORA:]
    q = _bdot((_rms_rows(cq) * gq_ref[...]).astype(BF16), wuq_ref[...])
    ckv_n = _rms_rows(ckv) * gkv_ref[...]

    @pl.when(jnp.logical_not(lat))
    def _state():
        ckv_ref[...] = ckv_n
        kr_ref[...] = krp[:, :MLA_ROPE]

    kv = _bdot(ckv_n.astype(BF16), wukv_ref[...])
    cos_t, sin_t = cos_ref[...], sin_ref[...]
    k_rope = jnp.where(lat, _rope_half32(krp, cos_t, sin_t), krp).astype(BF16)
    qs = MLA_SCALE * LOG2E
    for hd in range(MLA_HEADS):
        c0 = hd * MLA_QK
        qn = (q[:, c0:c0 + 128] * qs).astype(BF16)
        qr = q[:, c0 + 128:c0 + 256] * qs
        qc_ref[:, c0:c0 + 128] = qn
        qc_ref[:, c0 + 128:c0 + 256] = qr.astype(BF16)
        ql_ref[:, c0:c0 + 128] = qn
        ql_ref[:, c0 + 128:c0 + 256] = _rope_half32(qr, cos_t, sin_t).astype(BF16)
        k_ref[:, c0:c0 + 128] = kv[:, c0:c0 + 128].astype(BF16)
        k_ref[:, c0 + 128:c0 + 256] = k_rope
        v_ref[:, hd * 128:(hd + 1) * 128] = kv[:, c0 + 128:c0 + 256].astype(BF16)


def _mla_proj(x, mods, n1g, wd, gq, wuq, gkv, wukv, cos_t, sin_t):
    hq = MLA_HEADS * MLA_QK
    tw = functools.partial(_tok, tb=TBP)
    rope_spec = pl.BlockSpec((TBP, 128), lambda b: (_lat_blk(b, TBP), 0))
    return pl.pallas_call(
        _mla_proj_kernel,
        out_shape=[jax.ShapeDtypeStruct((N_TOK, hq), BF16),
                   jax.ShapeDtypeStruct((N_LAT_TOK, hq), BF16),
                   jax.ShapeDtypeStruct((N_TOK, hq), BF16),
                   jax.ShapeDtypeStruct((N_TOK, MLA_HEADS * 128), BF16),
                   jax.ShapeDtypeStruct((N_CTX_TOK, MLA_KV_LORA), F32),
                   jax.ShapeDtypeStruct((N_CTX_TOK, MLA_ROPE), F32)],
        grid=(N_TOK // TBP,),
        in_specs=[tw(D), _mods_spec(1, tb=TBP), _full((1, D)), _full((D, MLA_DOWN_PAD)),
                  _full((1, MLA_Q_LORA)), _full((MLA_Q_LORA, hq)), _full((1, MLA_KV_LORA)),
                  _full((MLA_KV_LORA, hq)), rope_spec, rope_spec],
        out_specs=[tw(hq), _lat_only(hq, TBP), tw(hq), tw(MLA_HEADS * 128),
                   _ctx_only(MLA_KV_LORA, TBP), _ctx_only(MLA_ROPE, TBP)],
        compiler_params=_cparams(1),
        name="mla_proj",
    )(x, mods, n1g, wd, gq, wuq, gkv, wukv, cos_t, sin_t)


def _mla_cache_kernel(ckv_ref, kr_ref, wukv_ref, k_ref, v_ref):
    kv = _bdot(ckv_ref[...].astype(BF16), wukv_ref[...])
    kr = kr_ref[...].astype(BF16)
    for hd in range(MLA_HEADS):
        c0 = hd * MLA_QK
        k_ref[:, c0:c0 + 128] = kv[:, c0:c0 + 128].astype(BF16)
        k_ref[:, c0 + 128:c0 + 256] = kr
        v_ref[:, hd * 128:(hd + 1) * 128] = kv[:, c0 + 128:c0 + 256].astype(BF16)


def _mla_cache(ckv, krp, wukv):
    n = ckv.shape[0]
    hq = MLA_HEADS * MLA_QK
    return pl.pallas_call(
        _mla_cache_kernel,
        out_shape=[jax.ShapeDtypeStruct((n, hq), BF16),
                   jax.ShapeDtypeStruct((n, MLA_HEADS * 128), BF16)],
        grid=(n // TB,),
        in_specs=[_tok(MLA_KV_LORA), _tok(128), _full((MLA_KV_LORA, hq))],
        out_specs=[_tok(hq), _tok(MLA_HEADS * 128)],
        compiler_params=_cparams(1),
        name="mla_cache_kv",
    )(ckv, krp, wukv)


def _attn_ctx_kernel(q_ref, k_ref, v_ref, o_ref, *, heads, kv_heads, dk):
    rep = heads // kv_heads
    for hd in range(heads):
        kh = hd // rep
        q = q_ref[:, hd * dk:(hd + 1) * dk]
        k = k_ref[:, kh * dk:(kh + 1) * dk]
        s = lax.dot_general(q, k, NT, preferred_element_type=F32)
        p = jnp.exp2(s - jnp.max(s, axis=-1, keepdims=True))
        l = jnp.sum(p, axis=-1, keepdims=True)
        o = _bdot(p.astype(BF16), v_ref[:, kh * 128:(kh + 1) * 128])
        o_ref[:, hd * 128:(hd + 1) * 128] = (o / l).astype(BF16)


def _attn_ctx(q, k, v, heads, kv_heads, dk, name):
    return pl.pallas_call(
        functools.partial(_attn_ctx_kernel, heads=heads, kv_heads=kv_heads, dk=dk),
        out_shape=jax.ShapeDtypeStruct((N_CTX_TOK, heads * 128), BF16),
        grid=(N_CTX_SEQ,),
        in_specs=[_tok(heads * dk), _tok(kv_heads * dk), _tok(kv_heads * 128)],
        out_specs=_tok(heads * 128),
        compiler_params=_cparams(1),
        name=name,
    )(q, k, v)


ATT_TQ = 2048


ATT_SUB = 256


def _attn_lat_kernel(qc_ref, ql_ref, kc_ref, kl_ref, vc_ref, vl_ref, o_ref):
    for i in range(ATT_TQ // ATT_SUB):
        rows = pl.ds(i * ATT_SUB, ATT_SUB)
        sc = lax.dot_general(qc_ref[rows, :], kc_ref[...], NT, preferred_element_type=F32)
        sl = lax.dot_general(ql_ref[rows, :], kl_ref[...], NT, preferred_element_type=F32)
        mx = jnp.maximum(jnp.max(sc, axis=-1, keepdims=True), jnp.max(sl, axis=-1, keepdims=True))
        pc = jnp.exp2(sc - mx)
        pn = jnp.exp2(sl - mx)
        l = jnp.sum(pc, axis=-1, keepdims=True) + jnp.sum(pn, axis=-1, keepdims=True)
        o = _bdot(pc.astype(BF16), vc_ref[...]) + _bdot(pn.astype(BF16), vl_ref[...])
        o_ref[rows, :] = (o / l).astype(BF16)


def _attn_lat(qc, ql, kc, kl, vc, vl, heads, kv_heads, dk, name):
    rep = heads // kv_heads
    nq = LAT_LEN // ATT_TQ
    ctx_q_blk = N_CTX_TOK // ATT_TQ
    ctx_kv_blk = N_CTX_TOK // LAT_LEN
    return pl.pallas_call(
        _attn_lat_kernel,
        out_shape=jax.ShapeDtypeStruct((N_LAT_TOK, heads * 128), BF16),
        grid=(N_LAT_SEQ, heads, nq),
        in_specs=[
            pl.BlockSpec((ATT_TQ, dk), lambda b, h, i: (ctx_q_blk + b * nq + i, h)),
            pl.BlockSpec((ATT_TQ, dk), lambda b, h, i: (b * nq + i, h)),
            pl.BlockSpec((PAST_LEN, dk), lambda b, h, i: (b, h // rep)),
            pl.BlockSpec((LAT_LEN, dk), lambda b, h, i: (ctx_kv_blk + b, h // rep)),
            pl.BlockSpec((PAST_LEN, 128), lambda b, h, i: (b, h // rep)),
            pl.BlockSpec((LAT_LEN, 128), lambda b, h, i: (ctx_kv_blk + b, h // rep)),
        ],
        out_specs=pl.BlockSpec((ATT_TQ, 128), lambda b, h, i: (b * nq + i, h)),
        compiler_params=_cparams(3),
        name=name,
    )(qc, ql, kc, kl, vc, vl)


def _oproj_kernel(ac_ref, al_ref, w_ref, x_ref, mods_ref,
                  n2g_ref, rwh_ref, rwl_ref, x1_ref, h2_ref, lg_ref):
    lat = pl.program_id(0) >= N_CTX_TOK // TBW
    a = jnp.where(lat, al_ref[...], ac_ref[...])
    y = _bdot(a, w_ref[...])
    _post(x_ref[...], y, mods_ref[0], n2g_ref, rwh_ref, rwl_ref, x1_ref, h2_ref, lg_ref)


def _oproj_layer(layer, a_ctx, a_lat, w_o, x, mods, n2g, rwh, rwl):
    return pl.pallas_call(
        _oproj_kernel,
        out_shape=_POST_OUT_SHAPES,
        grid=(N_TOK // TBW,),
        in_specs=[_ctx_only(D, TBW), _lat_only(D, TBW), _full((D, D)), _tok(D, tb=TBW),
                  _mods_spec(layer, tb=TBW)] + _POST_IN,
        out_specs=_post_out_specs(TBW),
        compiler_params=_cparams(1),
        name="attn_out_proj",
    )(a_ctx, a_lat, w_o, x, mods, n2g, rwh, rwl)


def _conv_in_kernel(x_ref, mods_ref, n1g_ref, w_ref, b_ref, u_ref):
    m = mods_ref[0]
    h = _modulate(x_ref[...], n1g_ref[...], m[:, 0:D], m[:, D:2 * D])
    u = _bdot(h.astype(BF16), w_ref[...]) + b_ref[...]
    u_ref[...] = u[:, :D] * jax.nn.sigmoid(u[:, D:])


def _conv_in(x, mods, n1g, w_pw1, b_pw1):
    return pl.pallas_call(
        _conv_in_kernel,
        out_shape=jax.ShapeDtypeStruct((N_TOK, D), F32),
        grid=(N_TOK // TBW,),
        in_specs=[_tok(D, tb=TBW), _mods_spec(2, tb=TBW), _full((1, D)), _full((D, 2 * D)),
                  _full((1, 2 * D))],
        out_specs=_tok(D, tb=TBW),
        compiler_params=_cparams(1),
        name="conv_pw1_glu",
    )(x, mods, n1g, w_pw1, b_pw1)


def _conv_out_kernel(u_ref, up_ref, un_ref, wdw_ref, bdw_ref, lng_ref, lnb_ref, w2_ref, b2_ref,
                     x_ref, mods_ref, n2g_ref, rwh_ref, rwl_ref, x1_ref, h2_ref, lg_ref):
    b = pl.program_id(0)
    has_prev, has_next = _seq_edges(b)
    up = jnp.where(has_prev, up_ref[...], 0.0)
    un = jnp.where(has_next, un_ref[...], 0.0)
    ucat = jnp.concatenate([up, u_ref[...], un], axis=0)
    ext = TB + 2 * HALO
    wdw = wdw_ref[...]
    half = CONV_WIDTH // 2
    shifted = [ucat if r == 0 else pltpu.roll(ucat, ext - r, axis=0) for r in range(8)]
    cols = []
    for c0 in range(0, D, 128):
        acc = jnp.zeros((TB, 128), F32)
        for k in range(CONV_WIDTH):
            a, r = divmod(k + HALO - half, 8)
            acc = acc + shifted[r][8 * a:8 * a + TB, c0:c0 + 128] * wdw[k:k + 1, c0:c0 + 128]
        cols.append(acc)
    u = jnp.concatenate(cols, axis=1) + bdw_ref[...]
    mu = jnp.mean(u, axis=-1, keepdims=True)
    uc = u - mu
    var = jnp.mean(uc * uc, axis=-1, keepdims=True)
    un_ = uc * lax.rsqrt(var + EPS) * lng_ref[...] + lnb_ref[...]
    y = _bdot(_silu(un_).astype(BF16), w2_ref[...]) + b2_ref[...]
    _post(x_ref[...], y, mods_ref[0], n2g_ref, rwh_ref, rwl_ref, x1_ref, h2_ref, lg_ref)


def _conv_out(u, w_dw, b_dw, ln_g, ln_b, w_pw2, b_pw2, x, mods, n2g, rwh, rwl):
    prev, nxt = _halo_specs(D)
    return pl.pallas_call(
        _conv_out_kernel,
        out_shape=_POST_OUT_SHAPES,
        grid=(N_BLK,),
        in_specs=[_tok(D), prev, nxt, _full((CONV_WIDTH, D)), _full((1, D)), _full((1, D)),
                  _full((1, D)), _full((D, D)), _full((1, D)), _tok(D), _mods_spec(2)] + _POST_IN,
        out_specs=_post_out_specs(),
        compiler_params=_cparams(1),
        name="conv_dw_ln_pw2",
    )(u, u, u, w_dw, b_dw, ln_g, ln_b, w_pw2, b_pw2, x, mods, n2g, rwh, rwl)


def _gqa_proj_kernel(x_ref, mods_ref, n1g_ref, w_ref, gq_ref, gk_ref, cos_ref, sin_ref,
                     qc_ref, ql_ref, kk_ref, vv_ref, kst_ref, vst_ref):
    lat = pl.program_id(0) >= N_CTX_TOK // TBP
    m = mods_ref[0]
    h = _modulate(x_ref[...], n1g_ref[...], m[:, 0:D], m[:, D:2 * D])
    qkv = _bdot(h.astype(BF16), w_ref[...])
    cos_t, sin_t = cos_ref[...], sin_ref[...]
    rope = lambda v: v * cos_t + pltpu.roll(v, 64, axis=1) * sin_t
    qs = GQA_SCALE * LOG2E
    for hd in range(GQA_HEADS):
        c0 = hd * 128
        qn = _rms_rows(qkv[:, c0:c0 + 128]) * gq_ref[...] * qs
        qc_ref[:, c0:c0 + 128] = qn.astype(BF16)
        ql_ref[:, c0:c0 + 128] = rope(qn).astype(BF16)
    kns = [_rms_rows(qkv[:, D + kh * 128:D + (kh + 1) * 128]) * gk_ref[...]
           for kh in range(GQA_KV_HEADS)]
    for kh in range(GQA_KV_HEADS):
        kk_ref[:, kh * 128:(kh + 1) * 128] = jnp.where(lat, rope(kns[kh]), kns[kh]).astype(BF16)
    v = qkv[:, D + 256:D + 512]
    vv_ref[...] = v.astype(BF16)

    @pl.when(jnp.logical_not(lat))
    def _state():
        for kh in range(GQA_KV_HEADS):
            kst_ref[:, kh * 128:(kh + 1) * 128] = kns[kh]
        vst_ref[...] = v


def _gqa_proj(x, mods, n1g, w_qkv, gq, gk, cos_t, sin_t):
    kvw = GQA_KV_HEADS * 128
    tw = functools.partial(_tok, tb=TBP)
    rope_spec = pl.BlockSpec((TBP, 128), lambda b: (_lat_blk(b, TBP), 0))
    return pl.pallas_call(
        _gqa_proj_kernel,
        out_shape=[jax.ShapeDtypeStruct((N_TOK, D), BF16),
                   jax.ShapeDtypeStruct((N_LAT_TOK, D), BF16),
                   jax.ShapeDtypeStruct((N_TOK, kvw), BF16),
                   jax.ShapeDtypeStruct((N_TOK, kvw), BF16),
                   jax.ShapeDtypeStruct((N_CTX_TOK, kvw), F32),
                   jax.ShapeDtypeStruct((N_CTX_TOK, kvw), F32)],
        grid=(N_TOK // TBP,),
        in_specs=[tw(D), _mods_spec(3, tb=TBP), _full((1, D)), _full((D, D + 2 * kvw)),
                  _full((1, 128)), _full((1, 128)), rope_spec, rope_spec],
        out_specs=[tw(D), _lat_only(D, TBP), tw(kvw), tw(kvw),
                   _ctx_only(kvw, TBP), _ctx_only(kvw, TBP)],
        compiler_params=_cparams(1),
        name="gqa_proj",
    )(x, mods, n1g, w_qkv, gq, gk, cos_t, sin_t)


REFINE_BITS = 15
F32_MIN_NORMAL = 2.0 ** -126


def _route_kernel(lg_ref, rel_ref, gate_ref, meta_ref, *, seg, n_units):
    n_tok = n_units * UNIT
    nseg = n_tok // seg
    n_chunks = n_tok // TB
    cap = seg // 8
    lg = lg_ref[...]
    ex = jnp.exp(lg - jnp.max(lg, axis=0, keepdims=True))
    aff = ex / jnp.sum(ex, axis=0, keepdims=True)
    affs = [aff[:, s * seg:(s + 1) * seg] for s in range(nseg)]
    col = lambda dtype: tuple(jnp.zeros((N_EXPERTS, 1), dtype) for _ in range(nseg))

    def count(mask):
        return jnp.sum(jnp.where(mask, 1.0, 0.0), axis=1, keepdims=True)

    def by_bits(i, bits):
        bit = jnp.left_shift(jnp.int32(1), 30 - i)
        out = []
        for s in range(nseg):
            cand = bits[s] | bit
            ok = count(affs[s] >= pltpu.bitcast(cand, F32)) >= cap
            out.append(jnp.where(ok, cand, bits[s]))
        return tuple(out)

    bits = lax.fori_loop(0, 31, by_bits, col(I32))
    encs = []
    for s in range(nseg):
        t = pltpu.bitcast(bits[s], F32)
        tn = pltpu.bitcast(bits[s] + 1, F32)
        inv = 1.0 / jnp.maximum(tn - t, F32_MIN_NORMAL)
        encs.append(jnp.where(affs[s] >= tn, 2.0, jnp.where(affs[s] >= t, (affs[s] - t) * inv, -1.0)))

    def refine(encs):
        def by_frac(i, cs):
            step = pltpu.bitcast(jnp.zeros((N_EXPERTS, 1), I32) + jnp.left_shift(126 - i, 23), F32)
            out = []
            for s in range(nseg):
                c_try = cs[s] + step
                out.append(jnp.where(count(encs[s] >= c_try) >= cap, c_try, cs[s]))
            return tuple(out)

        cs = lax.fori_loop(0, REFINE_BITS, by_frac, col(F32))
        w = 2.0 ** -REFINE_BITS
        return [jnp.where(e >= c + w, 2.0, jnp.where(e >= c, (e - c) * (1.0 / w), -1.0))
                for e, c in zip(encs, cs)]

    encs = refine(refine(encs))
    enc = jnp.concatenate(encs, axis=1) if nseg > 1 else encs[0]
    need = [cap - count(e >= 1.5) for e in encs]

    r = lax.broadcasted_iota(I32, (TB, TB), 0)
    c = lax.broadcasted_iota(I32, (TB, TB), 1)
    tri = jnp.where(r < c, 1.0, 0.0).astype(BF16)
    chunk = lambda v, cb: v[:, cb * TB:(cb + 1) * TB]
    stack = lambda masks: jnp.concatenate(
        [jnp.where(mk, 1.0, 0.0).astype(BF16) for mk in masks], axis=0)
    rows = lambda v, cb: v[cb * N_EXPERTS:(cb + 1) * N_EXPERTS]

    above = [chunk(enc, cb) >= 1.5 for cb in range(n_chunks)]
    tie = [(chunk(enc, cb) >= 0.0) & (chunk(enc, cb) < 1.5) for cb in range(n_chunks)]
    tie_rank = _bdot(stack(tie), tri)
    sel = []
    tie_carry = None
    for cb in range(n_chunks):
        if (cb * TB) % seg == 0:
            tie_carry = jnp.zeros((N_EXPERTS, 1), F32)
        keep = tie[cb] & (rows(tie_rank, cb) + tie_carry < need[(cb * TB) // seg])
        sel.append(above[cb] | keep)
        tie_carry = tie_carry + count(tie[cb])
    pos_local = _bdot(stack(sel), tri)

    lane = lax.broadcasted_iota(I32, (N_EXPERTS, 128), 1)
    lane8 = lax.broadcasted_iota(I32, (8, 128), 1)
    units_per_grp = GRP_BLK // UNIT_BLK
    for u in range(n_units):
        base = (u % units_per_grp) * (GRP_ROWS // units_per_grp)
        sel_carry = jnp.zeros((N_EXPERTS, 1), F32)
        meta_off = jnp.zeros((N_EXPERTS, 128), I32)
        meta_np = jnp.zeros((8, 128), I32)
        for j in range(UNIT_BLK):
            cb = u * UNIT_BLK + j
            pos = (rows(pos_local, cb) + sel_carry).astype(I32) + base
            off16 = (sel_carry.astype(I32) + base) & ~15
            sel_carry = sel_carry + count(sel[cb])
            rel = jnp.where(sel[cb], pos - off16, -1)
            rel_ref[:, cb * TB:(cb + 1) * TB] = rel
            gate_ref[:, cb * TB:(cb + 1) * TB] = jnp.where(sel[cb], chunk(aff, cb), 0.0)
            mx = jnp.max(jnp.max(rel, axis=1, keepdims=True), axis=0, keepdims=True)
            npass = jnp.right_shift(mx + WIN, 6)
            meta_off = jnp.where(lane == j, off16, meta_off)
            meta_np = jnp.where(lane8 == j, npass, meta_np)
        meta_ref[u, 0:N_EXPERTS, :] = meta_off
        meta_ref[u, N_EXPERTS:N_EXPERTS + 8, :] = meta_np


def _route(logits, seg, name):
    n = logits.shape[1]
    n_units = n // UNIT
    return pl.pallas_call(
        functools.partial(_route_kernel, seg=seg, n_units=n_units),
        out_shape=[jax.ShapeDtypeStruct((N_EXPERTS, n), I32),
                   jax.ShapeDtypeStruct((N_EXPERTS, n), F32),
                   jax.ShapeDtypeStruct((n_units, N_EXPERTS + 8, 128), I32)],
        grid=(1,),
        in_specs=[pl.BlockSpec((N_EXPERTS, n), lambda i: (0, 0))],
        out_specs=[pl.BlockSpec((N_EXPERTS, n), lambda i: (0, 0)),
                   pl.BlockSpec((N_EXPERTS, n), lambda i: (0, 0)),
                   pl.BlockSpec((n_units, N_EXPERTS + 8, 128), lambda i: (0, 0, 0))],
        compiler_params=_cparams(1),
        name=name,
    )(logits)


def _moe_kernel(off_ref, np_ref, h2_ref, rel_ref, gt_ref, wg_ref, wu_ref, wd_ref, x1_ref,
                mods_ref, fg_ref, o_ref, xy_ref, *, final, g0):
    g = pl.program_id(0) + g0
    s = pl.program_id(1)

    def window_start(e, blk, p):
        return pl.multiple_of(off_ref[(g * N_EXPERTS + e) * GRP_BLK + blk] + p * WIN, 16)

    def passes(blk):
        return np_ref[g * GRP_BLK + blk]

    def one_hot(rel, p, gates):
        rows = lax.broadcasted_iota(I32, (WIN, TB), 0) + p * WIN
        parts = []
        for e in range(N_EXPERTS):
            hit = jnp.broadcast_to(rel[e:e + 1, :], (WIN, TB)) == rows
            val = 1.0 if gates is None else jnp.broadcast_to(gates[e:e + 1, :], (WIN, TB))
            parts.append(jnp.where(hit, val, 0.0).astype(BF16))
        return jnp.concatenate(parts, axis=0)

    @pl.when(s == 0)
    def _zero():
        xy_ref[...] = jnp.zeros(xy_ref.shape, BF16)

    @pl.when(s < MOE_TS)
    def _gather():
        for j in range(MOE_BPS):
            blk = s * MOE_BPS + j
            rel = rel_ref[:, j * TB:(j + 1) * TB]
            hb = h2_ref[j * TB:(j + 1) * TB, :]

            def one_pass(p):
                win = _bdot(one_hot(rel, p, None), hb).astype(BF16)
                for e in range(N_EXPERTS):
                    rows = pl.ds(window_start(e, blk, p), WIN)
                    xy_ref[e, rows, :] = xy_ref[e, rows, :] + win[e * WIN:(e + 1) * WIN]

            one_pass(0)

            def extra(p, carry):
                one_pass(p)
                return carry

            lax.fori_loop(1, passes(blk), extra, 0)

    @pl.when((s >= MOE_TS) & (s < MOE_TS + N_EXPERTS))
    def _ffn():
        e = s - MOE_TS
        x = xy_ref[e, pl.ds(0, GRP_ROWS), :]
        a = _bdot(x, wg_ref[0])
        hid = (_silu(a) * _bdot(x, wu_ref[0])).astype(BF16)
        xy_ref[e, pl.ds(0, GRP_ROWS), :] = _bdot(hid, wd_ref[0]).astype(BF16)

    @pl.when(s >= MOE_TS + N_EXPERTS)
    def _combine():
        gate2 = mods_ref[0][:, 5 * D:6 * D]
        for j in range(MOE_BPS):
            blk = (s - (MOE_TS + N_EXPERTS)) * MOE_BPS + j
            rel = rel_ref[:, j * TB:(j + 1) * TB]
            gates = gt_ref[:, j * TB:(j + 1) * TB]

            def one_pass(p):
                wins = [xy_ref[e, pl.ds(window_start(e, blk, p), WIN), :] for e in range(N_EXPERTS)]
                return lax.dot_general(one_hot(rel, p, gates), jnp.concatenate(wins, axis=0), TN,
                                       preferred_element_type=F32)

            y = lax.fori_loop(1, passes(blk), lambda p, acc: acc + one_pass(p), one_pass(0))
            out = x1_ref[j * TB:(j + 1) * TB, :] + gate2 * y
            if final:
                out = _rms_rows(out) * fg_ref[...]
            o_ref[j * TB:(j + 1) * TB, :] = out


def _moe_layer(layer, off16, npass, h2, rel, gates, wg, wu, wd, x1, mods, final_g, final,
               g0=0, n_grp=N_GRP):
    n_steps = 2 * MOE_TS + N_EXPERTS
    tbs = MOE_BPS * TB

    def gather_blk(g, s, *_):
        return (g0 + g) * MOE_TS + jnp.minimum(s, MOE_TS - 1)

    def route_blk(g, s, *_):
        return (g0 + g) * MOE_TS + jnp.where(s < MOE_TS + N_EXPERTS, jnp.minimum(s, MOE_TS - 1),
                                             s - (MOE_TS + N_EXPERTS))

    def combine_blk(g, s, *_):
        return (g0 + g) * MOE_TS + jnp.maximum(s - (MOE_TS + N_EXPERTS), 0)

    def expert(g, s, *_):
        return jnp.clip(s - MOE_TS, 0, N_EXPERTS - 1)

    grid_spec = pltpu.PrefetchScalarGridSpec(
        num_scalar_prefetch=2,
        grid=(n_grp, n_steps),
        in_specs=[
            pl.BlockSpec((tbs, D), lambda *a: (gather_blk(*a), 0)),
            pl.BlockSpec((N_EXPERTS, tbs), lambda *a: (0, route_blk(*a))),
            pl.BlockSpec((N_EXPERTS, tbs), lambda *a: (0, route_blk(*a))),
            pl.BlockSpec((1, D, EXPERT_FF), lambda *a: (expert(*a), 0, 0)),
            pl.BlockSpec((1, D, EXPERT_FF), lambda *a: (expert(*a), 0, 0)),
            pl.BlockSpec((1, EXPERT_FF, D), lambda *a: (expert(*a), 0, 0)),
            pl.BlockSpec((tbs, D), lambda *a: (combine_blk(*a), 0)),
            pl.BlockSpec((1, 1, 6 * D),
                         lambda *a: (layer * N_COND + _cond_row(combine_blk(*a), tbs), 0, 0)),
            pl.BlockSpec((1, D), lambda *a: (0, 0)),
        ],
        out_specs=pl.BlockSpec((tbs, D), lambda *a: (combine_blk(*a) - g0 * MOE_TS, 0)),
        scratch_shapes=[pltpu.VMEM((N_EXPERTS, XY_ROWS, D), BF16)],
    )
    return pl.pallas_call(
        functools.partial(_moe_kernel, final=final, g0=g0),
        out_shape=jax.ShapeDtypeStruct((n_grp * GRP_BLK * TB, D), F32),
        grid_spec=grid_spec,
        compiler_params=_cparams(2),
        name="moe_experts",
    )(off16, npass, h2, rel, gates, wg, wu, wd, x1, mods, final_g)


def _moe(layer, x1, h2, logits, wg, wu, wd, mods, final_g, final):
    rel_c, gate_c, meta_c = _route(logits[:, :N_CTX_TOK], CTX_LEN, "route_ctx")
    rel_l, gate_l, meta_l = _route(logits[:, N_CTX_TOK:], LAT_LEN, "route_lat")
    rel = jnp.concatenate([rel_c, rel_l], axis=1)
    gates = jnp.concatenate([gate_c, gate_l], axis=1)
    meta = jnp.concatenate([meta_c, meta_l], axis=0)
    upg = GRP_BLK // UNIT_BLK
    off16 = meta[:, :N_EXPERTS, :UNIT_BLK].reshape(N_GRP, upg, N_EXPERTS, UNIT_BLK)
    off16 = jnp.swapaxes(off16, 1, 2).reshape(-1)
    npass = meta[:, N_EXPERTS, :UNIT_BLK].reshape(-1)
    args = (layer, off16, npass, h2, rel, gates, wg, wu, wd, x1, mods, final_g, final)
    if not final:
        return _moe_layer(*args)
    ctx_grp = N_CTX_BLK // GRP_BLK
    return _moe_layer(*args, g0=0, n_grp=ctx_grp), _moe_layer(*args, g0=ctx_grp, n_grp=N_GRP - ctx_grp)


def _axial_tables(rot_dim):
    t = jnp.arange(LAT_LEN)
    row = (t // GRID_W).astype(F32)
    col = (t % GRID_W).astype(F32)
    axis_dim = rot_dim // 2
    inv = ROPE_THETA ** (-jnp.arange(0, axis_dim, 2, dtype=F32) / axis_dim)
    ang = jnp.concatenate([row[:, None] * inv, col[:, None] * inv], axis=-1)
    return jnp.cos(ang), jnp.sin(ang)


def kernel(x_prompt, x_sample, c, cache_mla_ckv, cache_mla_krope, cache_gqa_k, cache_gqa_v, c_ctx, ada_w, ada_b, norm1_g, norm2_g, final_g, pool_w, pool_scale, mla_w_down, mla_g_q, mla_w_uq, mla_g_kv, mla_w_ukv, mla_w_o, conv_w_pw1, conv_b_pw1, conv_w_dw, conv_b_dw, conv_ln_g, conv_ln_b, conv_w_pw2, conv_b_pw2, gqa_w_qkv, gqa_g_q, gqa_g_k, gqa_w_o, router_w, moe_w_gate, moe_w_up, moe_w_down):
    cond =jnp.zeros((N_COND, D), F32).at[0].set(c_ctx).at[1:1 + N_LAT_SEQ].set(c)
    mods = _adaln(cond, ada_w, ada_b).reshape(DEPTH * N_COND, 1, 6 * D)

    row = lambda v: v.reshape(1, -1)
    rw_t = jnp.swapaxes(router_w, 1, 2)
    rwh = rw_t.astype(BF16)
    rwl = (rw_t - rwh.astype(F32)).astype(BF16)
    wg, wu, wd = moe_w_gate.astype(BF16), moe_w_up.astype(BF16), moe_w_down.astype(BF16)
    fg = row(final_g)

    def post_args(l):
        return row(norm2_g[l]), rwh[l], rwl[l]

    def moe(l, x1, h2, lg):
        return _moe(l, x1, h2, lg, wg[l], wu[l], wd[l], mods, fg, l == DEPTH - 1)

    x1, h2, lg = _pool_layer(x_prompt.reshape(N_CTX_TOK, D), x_sample.reshape(N_LAT_TOK, D), mods,
                             row(norm1_g[0]), pool_w[0].astype(BF16), row(pool_scale[0]),
                             *post_args(0))
    x = moe(0, x1, h2, lg)

    cos, sin = _axial_tables(MLA_ROPE)
    z = jnp.zeros((LAT_LEN, 64), F32)
    cos_t = jnp.concatenate([cos, cos, z], axis=1)
    sin_t = jnp.concatenate([-sin, sin, z], axis=1)
    w_down = jnp.pad(mla_w_down[0], ((0, 0), (0, MLA_DOWN_PAD - mla_w_down.shape[2]))).astype(BF16)
    w_uq = jnp.pad(mla_w_uq[0].reshape(MLA_Q_LORA, MLA_HEADS, MLA_NOPE + MLA_ROPE),
                   ((0, 0), (0, 0), (0, MLA_QK - MLA_NOPE - MLA_ROPE)))
    w_uq = w_uq.reshape(MLA_Q_LORA, MLA_HEADS * MLA_QK).astype(BF16)
    w_ukv = mla_w_ukv[0].astype(BF16)
    qc, ql, kmat, vmat, ckv_n, krp = _mla_proj(
        x, mods, row(norm1_g[1]), w_down, row(mla_g_q[0]), w_uq, row(mla_g_kv[0]), w_ukv, cos_t, sin_t)
    cache_kr = jnp.pad(cache_mla_krope[:, 0].reshape(N_LAT_SEQ * PAST_LEN, MLA_ROPE),
                       ((0, 0), (0, 128 - MLA_ROPE)))
    kc, vc = _mla_cache(cache_mla_ckv[:, 0].reshape(N_LAT_SEQ * PAST_LEN, MLA_KV_LORA), cache_kr, w_ukv)
    a_ctx = _attn_ctx(qc, kmat, vmat, MLA_HEADS, MLA_HEADS, MLA_QK, "mla_attn_ctx")
    a_lat = _attn_lat(qc, ql, kc, kmat, vc, vmat, MLA_HEADS, MLA_HEADS, MLA_QK, "mla_attn_lat")
    x1, h2, lg = _oproj_layer(1, a_ctx, a_lat, mla_w_o[0].astype(BF16), x, mods, *post_args(1))
    x = moe(1, x1, h2, lg)
    state_ckv = ckv_n.reshape(N_CTX_SEQ, 1, CTX_LEN, MLA_KV_LORA)
    state_kr = krp.reshape(N_CTX_SEQ, 1, CTX_LEN, MLA_ROPE)

    u = _conv_in(x, mods, row(norm1_g[2]), conv_w_pw1[0].astype(BF16), row(conv_b_pw1[0]))
    x1, h2, lg = _conv_out(u, conv_w_dw[0], row(conv_b_dw[0]), row(conv_ln_g[0]), row(conv_ln_b[0]),
                           conv_w_pw2[0].astype(BF16), row(conv_b_pw2[0]), x, mods, *post_args(2))
    x = moe(2, x1, h2, lg)

    cos, sin = _axial_tables(GQA_HEAD_DIM)
    cos_t = jnp.concatenate([cos, cos], axis=1)
    sin_t = jnp.concatenate([-sin, sin], axis=1)
    qc, ql, kk, vv, kst, vst = _gqa_proj(x, mods, row(norm1_g[3]), gqa_w_qkv[0].astype(BF16),
                                         row(gqa_g_q[0]), row(gqa_g_k[0]), cos_t, sin_t)
    kvw = GQA_KV_HEADS * GQA_HEAD_DIM
    kc = cache_gqa_k[:, 0].reshape(N_LAT_SEQ * PAST_LEN, kvw).astype(BF16)
    vc = cache_gqa_v[:, 0].reshape(N_LAT_SEQ * PAST_LEN, kvw).astype(BF16)
    a_ctx = _attn_ctx(qc, kk, vv, GQA_HEADS, GQA_KV_HEADS, GQA_HEAD_DIM, "gqa_attn_ctx")
    a_lat = _attn_lat(qc, ql, kc, kk, vc, vv, GQA_HEADS, GQA_KV_HEADS, GQA_HEAD_DIM, "gqa_attn_lat")
    x1, h2, lg = _oproj_layer(3, a_ctx, a_lat, gqa_w_o[0].astype(BF16), x, mods, *post_args(3))
    y_ctx, y_lat = moe(3, x1, h2, lg)
    state_k = kst.reshape(N_CTX_SEQ, 1, CTX_LEN, GQA_KV_HEADS, GQA_HEAD_DIM)
    state_v = vst.reshape(N_CTX_SEQ, 1, CTX_LEN, GQA_KV_HEADS, GQA_HEAD_DIM)

    y_prompt = y_ctx.reshape(N_CTX_SEQ, CTX_LEN, D)
    y_sample = y_lat.reshape(N_LAT_SEQ, LAT_LEN, D)
    return (y_prompt, y_sample, state_ckv, state_kr, state_k, state_v)
```

```python
import functools
import math

import jax
import jax.numpy as jnp
from jax import lax
from jax.experimental import pallas as pl
from jax.experimental.pallas import tpu as pltpu

F32 = jnp.float32
BF16 = jnp.bfloat16
I32 = jnp.int32

D = 1024
DEPTH = 4
EPS = 1e-6
GRID_W = 64
ROPE_THETA = 10000.0
LOG2E = math.log2(math.e)

TB = 256
TBW = 512
TBP = 256
HALO = 16
N_CTX_SEQ, CTX_LEN = 32, 256
N_LAT_SEQ, LAT_LEN = 8, 2048
PAST_LEN = 512
N_CTX_TOK = N_CTX_SEQ * CTX_LEN
N_LAT_TOK = N_LAT_SEQ * LAT_LEN
N_TOK = N_CTX_TOK + N_LAT_TOK
N_CTX_BLK = N_CTX_TOK // TB
LAT_BLK = LAT_LEN // TB
N_BLK = N_TOK // TB
N_COND = 16

POOL_WINDOWS = (2, 4, 8, 16)
POOL_GW = 256
MLA_HEADS = 8
MLA_Q_LORA = 384
MLA_KV_LORA = 256
MLA_NOPE = 128
MLA_ROPE = 64
MLA_DOWN_PAD = 768
MLA_QK = 256
MLA_SCALE = (MLA_NOPE + MLA_ROPE) ** -0.5
CONV_WIDTH = 31
GQA_HEADS = 8
GQA_KV_HEADS = 2
GQA_HEAD_DIM = 128
GQA_SCALE = GQA_HEAD_DIM ** -0.5
N_EXPERTS = 16
EXPERT_FF = 512

UNIT = 2048
UNIT_BLK = UNIT // TB
GRP_BLK = 16
N_GRP = N_BLK // GRP_BLK
GRP_ROWS = 512
WIN = 48
MAX_PASSES = -(-(TB + 15) // WIN)
XY_ROWS = GRP_ROWS + WIN
MOE_BPS = 2
MOE_TS = GRP_BLK // MOE_BPS

NT = (((1,), (1,)), ((), ()))
TN = (((0,), (0,)), ((), ()))

VMEM_LIMIT = 56 * 1024 * 1024


def _cparams(n_axes):
    return pltpu.CompilerParams(dimension_semantics=("arbitrary",) * n_axes,
                                vmem_limit_bytes=VMEM_LIMIT)


def _cond_row(b, tb=TB):
    n_ctx = N_CTX_TOK // tb
    return jnp.where(b < n_ctx, 0, 1 + (b - n_ctx) // (LAT_LEN // tb))


def _lat_blk(b, tb=TB):
    n_ctx = N_CTX_TOK // tb
    return jnp.where(b < n_ctx, 0, (b - n_ctx) % (LAT_LEN // tb))


def _mods_spec(layer, blk_of=lambda *g: g[0], tb=TB):
    return pl.BlockSpec((1, 1, 6 * D),
                        lambda *g: (layer * N_COND + _cond_row(blk_of(*g), tb), 0, 0))


def _full(shape):
    return pl.BlockSpec(shape, lambda *g: (0,) * len(shape))


def _tok(width, blk_of=lambda *g: g[0], tb=TB):
    return pl.BlockSpec((tb, width), lambda *g: (blk_of(*g), 0))


def _ctx_only(width, tb=TB):
    return pl.BlockSpec((tb, width), lambda b: (jnp.minimum(b, N_CTX_TOK // tb - 1), 0))


def _lat_only(width, tb=TB):
    return pl.BlockSpec((tb, width), lambda b: (jnp.maximum(b - N_CTX_TOK // tb, 0), 0))


def _halo_specs(width):
    per = TB // HALO
    prev = pl.BlockSpec((HALO, width), lambda b: (jnp.maximum(b * per - 1, 0), 0))
    nxt = pl.BlockSpec((HALO, width), lambda b: (jnp.minimum((b + 1) * per, N_TOK // HALO - 1), 0))
    return prev, nxt


def _seq_edges(b):
    lat = b >= N_CTX_BLK
    j = _lat_blk(b)
    return lat & (j > 0), lat & (j < LAT_BLK - 1)


def _rms_rows(x):
    return x * lax.rsqrt(jnp.mean(x * x, axis=-1, keepdims=True) + EPS)


def _modulate(x, g, shift, scale):
    return _rms_rows(x) * g * (1.0 + scale) + shift


def _silu(x):
    return x * jax.nn.sigmoid(x)


def _bdot(a, b):
    return jnp.dot(a, b, preferred_element_type=F32)


def _adaln_kernel(c_ref, w_ref, b_ref, o_ref):
    s = _silu(c_ref[...]).astype(BF16)
    o_ref[0] = _bdot(s, w_ref[0].astype(BF16)) + b_ref[0]


def _adaln(cond, ada_w, ada_b):
    tn = 1536
    return pl.pallas_call(
        _adaln_kernel,
        out_shape=jax.ShapeDtypeStruct((DEPTH, N_COND, 6 * D), F32),
        grid=(DEPTH, 6 * D // tn),
        in_specs=[
            _full((N_COND, D)),
            pl.BlockSpec((1, D, tn), lambda l, n: (l, 0, n)),
            pl.BlockSpec((1, 1, tn), lambda l, n: (l, 0, n)),
        ],
        out_specs=pl.BlockSpec((1, N_COND, tn), lambda l, n: (l, 0, n)),
        compiler_params=_cparams(2),
        name="adaln",
    )(cond, ada_w, ada_b.reshape(DEPTH, 1, 6 * D))


def _post(x, y, m, n2g_ref, rw_ref, x1_ref, h2_ref, lg_ref, rows=slice(None)):
    x1 = x + m[:, 2 * D:3 * D] * y
    x1_ref[rows, :] = x1
    h2 = _modulate(x1, n2g_ref[...], m[:, 3 * D:4 * D], m[:, 4 * D:5 * D])
    hh = h2.astype(BF16)
    hl = (h2 - hh.astype(F32)).astype(BF16)
    h2_ref[rows, :] = hh
    n = x.shape[0]
    r = lax.dot_general(rw_ref[...], jnp.concatenate([hh, hl], axis=0), NT,
                        preferred_element_type=F32)
    lg_ref[:, rows] = r[:N_EXPERTS, :n] + r[N_EXPERTS:, :n] + r[:N_EXPERTS, n:]


_POST_IN = [_full((1, D)), _full((2 * N_EXPERTS, D))]
SUB = 256
_POST_OUT_SHAPES = [jax.ShapeDtypeStruct((N_TOK, D), F32),
                    jax.ShapeDtypeStruct((N_TOK, D), BF16),
                    jax.ShapeDtypeStruct((N_EXPERTS, N_TOK), F32)]


def _post_out_specs(tb=TB):
    return [_tok(D, tb=tb), _tok(D, tb=tb), pl.BlockSpec((N_EXPERTS, tb), lambda b: (0, b))]


def _pool_kernel(xc_ref, xl_ref, xp_ref, xn_ref, mods_ref, n1g_ref, pw_ref, ps_ref,
                 n2g_ref, rw_ref, x1_ref, h2_ref, lg_ref):
    b = pl.program_id(0)
    has_prev, has_next = _seq_edges(b)
    m = mods_ref[0]
    g, sh, sc = n1g_ref[...], m[:, 0:D], m[:, D:2 * D]
    x = jnp.where(b >= N_CTX_BLK, xl_ref[...], xc_ref[...])
    hc = _modulate(x, g, sh, sc)
    hext = jnp.concatenate([_modulate(xp_ref[...], g, sh, sc), hc,
                            _modulate(xn_ref[...], g, sh, sc)], axis=0)
    ext = TB + 2 * HALO
    t = lax.broadcasted_iota(I32, (TB, ext), 0)
    u = lax.broadcasted_iota(I32, (TB, ext), 1) - HALO
    valid = ((u >= 0) | has_prev) & ((u < TB) | has_next)
    tc = lax.broadcasted_iota(I32, (TB, 1), 0)
    lo_lim = jnp.where(has_prev, -HALO, 0)
    hi_lim = jnp.where(has_next, TB + HALO, TB)
    ys = []
    for gi, win in enumerate(POOL_WINDOWS):
        hw = win // 2
        band = jnp.where((u >= t - hw) & (u < t + hw) & valid, 1.0, 0.0).astype(BF16)
        cnt = (jnp.minimum(tc + hw, hi_lim) - jnp.maximum(tc - hw, lo_lim)).astype(F32)
        hg = hext[:, gi * POOL_GW:(gi + 1) * POOL_GW]
        hi = hg.astype(BF16)
        lo = (hg - hi.astype(F32)).astype(BF16)
        tot = _bdot(band, hi) + _bdot(band, lo)
        pooled = tot / cnt - hc[:, gi * POOL_GW:(gi + 1) * POOL_GW]
        ys.append(_bdot(pooled.astype(BF16), pw_ref[gi]))
    y = jnp.concatenate(ys, axis=1) * ps_ref[...]
    _post(x, y, m, n2g_ref, rw_ref, x1_ref, h2_ref, lg_ref)


def _pool_layer(x_ctx, x_lat, mods, n1g, pool_w, pool_scale, n2g, rw):
    per = TB // HALO
    lat_b = lambda b: jnp.maximum(b - N_CTX_BLK, 0)
    prev = pl.BlockSpec((HALO, D), lambda b: (jnp.maximum(lat_b(b) * per - 1, 0), 0))
    nxt = pl.BlockSpec((HALO, D),
                       lambda b: (jnp.minimum((lat_b(b) + 1) * per, N_LAT_TOK // HALO - 1), 0))
    return pl.pallas_call(
        _pool_kernel,
        out_shape=_POST_OUT_SHAPES,
        grid=(N_BLK,),
        in_specs=[_ctx_only(D), _lat_only(D), prev, nxt, _mods_spec(0), _full((1, D)),
                  _full((4, POOL_GW, POOL_GW)), _full((1, D))] + _POST_IN,
        out_specs=_post_out_specs(),
        compiler_params=_cparams(1),
        name="pool_mixer",
    )(x_ctx, x_lat, x_lat, x_lat, mods, n1g, pool_w, pool_scale, n2g, rw)


def _rope_half32(v, cos_t, sin_t):
    lane = lax.broadcasted_iota(I32, v.shape, 1)
    swapped = jnp.where(lane < 32, pltpu.roll(v, 96, axis=1), pltpu.roll(v, 32, axis=1))
    return v * cos_t + swapped * sin_t


def _mla_proj_kernel(x_ref, mods_ref, n1g_ref, wd_ref, gq_ref, wuq_ref, gkv_ref, wukv_ref,
                     cos_ref, sin_ref, qc_ref, ql_ref, k_ref, v_ref, ckv_ref, kr_ref):
    lat = pl.program_id(0) >= N_CTX_TOK // TBP
    m = mods_ref[0]
    h = _modulate(x_ref[...], n1g_ref[...], m[:, 0:D], m[:, D:2 * D])
    d = _bdot(h.astype(BF16), wd_ref[...])
    cq = d[:, :MLA_Q_LORA]
    ckv = d[:, MLA_Q_LORA:MLA_Q_LORA + MLA_KV_LORA]
    krp = d[:, MLA_Q_LORA + MLA_KV_LORA:]
    q = _bdot((_rms_rows(cq) * gq_ref[...]).astype(BF16), wuq_ref[...])
    ckv_n = _rms_rows(ckv) * gkv_ref[...]
    kv = _bdot(ckv_n.astype(BF16), wukv_ref[...])
    cos_t, sin_t = cos_ref[...], sin_ref[...]
    k_rope = jnp.where(lat, _rope_half32(krp, cos_t, sin_t), krp).astype(BF16)
    qs = MLA_SCALE * LOG2E
    for hd in range(MLA_HEADS):
        c0 = hd * MLA_QK
        qn = (q[:, c0:c0 + 128] * qs).astype(BF16)
        qr = q[:, c0 + 128:c0 + 256] * qs
        qc_ref[:, c0:c0 + 128] = qn
        qc_ref[:, c0 + 128:c0 + 256] = qr.astype(BF16)
        ql_ref[:, c0:c0 + 128] = qn
        ql_ref[:, c0 + 128:c0 + 256] = _rope_half32(qr, cos_t, sin_t).astype(BF16)
        k_ref[:, c0:c0 + 128] = kv[:, c0:c0 + 128].astype(BF16)
        k_ref[:, c0 + 128:c0 + 256] = k_rope
        v_ref[:, hd * 128:(hd + 1) * 128] = kv[:, c0 + 128:c0 + 256].astype(BF16)

    @pl.when(jnp.logical_not(lat))
    def _state():
        ckv_ref[...] = ckv_n
        kr_ref[...] = krp[:, :MLA_ROPE]


def _mla_proj(x, mods, n1g, wd, gq, wuq, gkv, wukv, cos_t, sin_t):
    hq = MLA_HEADS * MLA_QK
    tw = functools.partial(_tok, tb=TBP)
    rope_spec = pl.BlockSpec((TBP, 128), lambda b: (_lat_blk(b, TBP), 0))
    return pl.pallas_call(
        _mla_proj_kernel,
        out_shape=[jax.ShapeDtypeStruct((N_TOK, hq), BF16),
                   jax.ShapeDtypeStruct((N_LAT_TOK, hq), BF16),
                   jax.ShapeDtypeStruct((N_TOK, hq), BF16),
                   jax.ShapeDtypeStruct((N_TOK, MLA_HEADS * 128), BF16),
                   jax.ShapeDtypeStruct((N_CTX_TOK, MLA_KV_LORA), F32),
                   jax.ShapeDtypeStruct((N_CTX_TOK, MLA_ROPE), F32)],
        grid=(N_TOK // TBP,),
        in_specs=[tw(D), _mods_spec(1, tb=TBP), _full((1, D)), _full((D, MLA_DOWN_PAD)),
                  _full((1, MLA_Q_LORA)), _full((MLA_Q_LORA, hq)), _full((1, MLA_KV_LORA)),
                  _full((MLA_KV_LORA, hq)), rope_spec, rope_spec],
        out_specs=[tw(hq), _lat_only(hq, TBP), tw(hq), tw(MLA_HEADS * 128),
                   _ctx_only(MLA_KV_LORA, TBP), _ctx_only(MLA_ROPE, TBP)],
        compiler_params=_cparams(1),
        name="mla_proj",
    )(x, mods, n1g, wd, gq, wuq, gkv, wukv, cos_t, sin_t)


def _mla_cache_kernel(ckv_ref, kr_ref, wukv_ref, k_ref, v_ref):
    kv = _bdot(ckv_ref[...].astype(BF16), wukv_ref[...])
    kr = kr_ref[...].astype(BF16)
    for hd in range(MLA_HEADS):
        c0 = hd * MLA_QK
        k_ref[:, c0:c0 + 128] = kv[:, c0:c0 + 128].astype(BF16)
        k_ref[:, c0 + 128:c0 + 256] = kr
        v_ref[:, hd * 128:(hd + 1) * 128] = kv[:, c0 + 128:c0 + 256].astype(BF16)


def _mla_cache(ckv, krp, wukv):
    n = ckv.shape[0]
    hq = MLA_HEADS * MLA_QK
    return pl.pallas_call(
        _mla_cache_kernel,
        out_shape=[jax.ShapeDtypeStruct((n, hq), BF16),
                   jax.ShapeDtypeStruct((n, MLA_HEADS * 128), BF16)],
        grid=(n // TB,),
        in_specs=[_tok(MLA_KV_LORA), _tok(128), _full((MLA_KV_LORA, hq))],
        out_specs=[_tok(hq), _tok(MLA_HEADS * 128)],
        compiler_params=_cparams(1),
        name="mla_cache_kv",
    )(ckv, krp, wukv)


def _attn_ctx_kernel(q_ref, k_ref, v_ref, o_ref, *, heads, kv_heads, dk):
    rep = heads // kv_heads
    for hd in range(heads):
        kh = hd // rep
        q = q_ref[:, hd * dk:(hd + 1) * dk]
        k = k_ref[:, kh * dk:(kh + 1) * dk]
        s = lax.dot_general(q, k, NT, preferred_element_type=F32)
        p = jnp.exp2(s - jnp.max(s, axis=-1, keepdims=True))
        l = jnp.sum(p, axis=-1, keepdims=True)
        o = _bdot(p.astype(BF16), v_ref[:, kh * 128:(kh + 1) * 128])
        o_ref[:, hd * 128:(hd + 1) * 128] = (o / l).astype(BF16)


def _attn_ctx(q, k, v, heads, kv_heads, dk, name):
    return pl.pallas_call(
        functools.partial(_attn_ctx_kernel, heads=heads, kv_heads=kv_heads, dk=dk),
        out_shape=jax.ShapeDtypeStruct((N_CTX_TOK, heads * 128), BF16),
        grid=(N_CTX_SEQ,),
        in_specs=[_tok(heads * dk), _tok(kv_heads * dk), _tok(kv_heads * 128)],
        out_specs=_tok(heads * 128),
        compiler_params=_cparams(1),
        name=name,
    )(q, k, v)


ATT_TQ = 2048


ATT_SUB = 256


def _attn_lat_kernel(qc_ref, ql_ref, kc_ref, kl_ref, vc_ref, vl_ref, o_ref):
    for i in range(ATT_TQ // ATT_SUB):
        rows = pl.ds(i * ATT_SUB, ATT_SUB)
        sc = lax.dot_general(qc_ref[rows, :], kc_ref[...], NT, preferred_element_type=F32)
        sl = lax.dot_general(ql_ref[rows, :], kl_ref[...], NT, preferred_element_type=F32)
        mx = jnp.maximum(jnp.max(sc, axis=-1, keepdims=True), jnp.max(sl, axis=-1, keepdims=True))
        pc = jnp.exp2(sc - mx)
        pn = jnp.exp2(sl - mx)
        l = jnp.sum(pc, axis=-1, keepdims=True) + jnp.sum(pn, axis=-1, keepdims=True)
        o = _bdot(pc.astype(BF16), vc_ref[...]) + _bdot(pn.astype(BF16), vl_ref[...])
        o_ref[rows, :] = (o / l).astype(BF16)


def _attn_lat(qc, ql, kc, kl, vc, vl, heads, kv_heads, dk, name):
    rep = heads // kv_heads
    nq = LAT_LEN // ATT_TQ
    ctx_q_blk = N_CTX_TOK // ATT_TQ
    ctx_kv_blk = N_CTX_TOK // LAT_LEN
    return pl.pallas_call(
        _attn_lat_kernel,
        out_shape=jax.ShapeDtypeStruct((N_LAT_TOK, heads * 128), BF16),
        grid=(N_LAT_SEQ, heads, nq),
        in_specs=[
            pl.BlockSpec((ATT_TQ, dk), lambda b, h, i: (ctx_q_blk + b * nq + i, h)),
            pl.BlockSpec((ATT_TQ, dk), lambda b, h, i: (b * nq + i, h)),
            pl.BlockSpec((PAST_LEN, dk), lambda b, h, i: (b, h // rep)),
            pl.BlockSpec((LAT_LEN, dk), lambda b, h, i: (ctx_kv_blk + b, h // rep)),
            pl.BlockSpec((PAST_LEN, 128), lambda b, h, i: (b, h // rep)),
            pl.BlockSpec((LAT_LEN, 128), lambda b, h, i: (ctx_kv_blk + b, h // rep)),
        ],
        out_specs=pl.BlockSpec((ATT_TQ, 128), lambda b, h, i: (b * nq + i, h)),
        compiler_params=_cparams(3),
        name=name,
    )(qc, ql, kc, kl, vc, vl)


def _oproj_kernel(ac_ref, al_ref, w_ref, x_ref, mods_ref,
                  n2g_ref, rw_ref, x1_ref, h2_ref, lg_ref):
    lat = pl.program_id(0) >= N_CTX_TOK // TBW
    a = jnp.where(lat, al_ref[...], ac_ref[...])
    y = _bdot(a, w_ref[...])
    _post(x_ref[...], y, mods_ref[0], n2g_ref, rw_ref, x1_ref, h2_ref, lg_ref)


def _oproj_layer(layer, a_ctx, a_lat, w_o, x, mods, n2g, rw):
    return pl.pallas_call(
        _oproj_kernel,
        out_shape=_POST_OUT_SHAPES,
        grid=(N_TOK // TBW,),
        in_specs=[_ctx_only(D, TBW), _lat_only(D, TBW), _full((D, D)), _tok(D, tb=TBW),
                  _mods_spec(layer, tb=TBW)] + _POST_IN,
        out_specs=_post_out_specs(TBW),
        compiler_params=_cparams(1),
        name="attn_out_proj",
    )(a_ctx, a_lat, w_o, x, mods, n2g, rw)


def _conv_in_kernel(x_ref, mods_ref, n1g_ref, w_ref, b_ref, u_ref):
    m = mods_ref[0]
    for i in range(TBW // SUB):
        rows = pl.ds(i * SUB, SUB)
        h = _modulate(x_ref[rows, :], n1g_ref[...], m[:, 0:D], m[:, D:2 * D])
        u = _bdot(h.astype(BF16), w_ref[...]) + b_ref[...]
        u_ref[rows, :] = u[:, :D] * jax.nn.sigmoid(u[:, D:])


def _conv_in(x, mods, n1g, w_pw1, b_pw1):
    return pl.pallas_call(
        _conv_in_kernel,
        out_shape=jax.ShapeDtypeStruct((N_TOK, D), F32),
        grid=(N_TOK // TBW,),
        in_specs=[_tok(D, tb=TBW), _mods_spec(2, tb=TBW), _full((1, D)), _full((D, 2 * D)),
                  _full((1, 2 * D))],
        out_specs=_tok(D, tb=TBW),
        compiler_params=_cparams(1),
        name="conv_pw1_glu",
    )(x, mods, n1g, w_pw1, b_pw1)


def _conv_out_kernel(u_ref, up_ref, un_ref, wdw_ref, bdw_ref, lng_ref, lnb_ref, w2_ref, b2_ref,
                     x_ref, mods_ref, n2g_ref, rw_ref, x1_ref, h2_ref, lg_ref):
    b = pl.program_id(0)
    has_prev, has_next = _seq_edges(b)
    up = jnp.where(has_prev, up_ref[...], 0.0)
    un = jnp.where(has_next, un_ref[...], 0.0)
    ucat = jnp.concatenate([up, u_ref[...], un], axis=0)
    ext = TB + 2 * HALO
    wdw = wdw_ref[...]
    half = CONV_WIDTH // 2
    shifted = [ucat if r == 0 else pltpu.roll(ucat, ext - r, axis=0) for r in range(8)]
    cols = []
    for c0 in range(0, D, 128):
        acc = jnp.zeros((TB, 128), F32)
        for k in range(CONV_WIDTH):
            a, r = divmod(k + HALO - half, 8)
            acc = acc + shifted[r][8 * a:8 * a + TB, c0:c0 + 128] * wdw[k:k + 1, c0:c0 + 128]
        cols.append(acc)
    u = jnp.concatenate(cols, axis=1) + bdw_ref[...]
    mu = jnp.mean(u, axis=-1, keepdims=True)
    uc = u - mu
    var = jnp.mean(uc * uc, axis=-1, keepdims=True)
    un_ = uc * lax.rsqrt(var + EPS) * lng_ref[...] + lnb_ref[...]
    y = _bdot(_silu(un_).astype(BF16), w2_ref[...]) + b2_ref[...]
    _post(x_ref[...], y, mods_ref[0], n2g_ref, rw_ref, x1_ref, h2_ref, lg_ref)


def _conv_out(u, w_dw, b_dw, ln_g, ln_b, w_pw2, b_pw2, x, mods, n2g, rw):
    prev, nxt = _halo_specs(D)
    return pl.pallas_call(
        _conv_out_kernel,
        out_shape=_POST_OUT_SHAPES,
        grid=(N_BLK,),
        in_specs=[_tok(D), prev, nxt, _full((CONV_WIDTH, D)), _full((1, D)), _full((1, D)),
                  _full((1, D)), _full((D, D)), _full((1, D)), _tok(D), _mods_spec(2)] + _POST_IN,
        out_specs=_post_out_specs(),
        compiler_params=_cparams(1),
        name="conv_dw_ln_pw2",
    )(u, u, u, w_dw, b_dw, ln_g, ln_b, w_pw2, b_pw2, x, mods, n2g, rw)


def _gqa_proj_kernel(x_ref, mods_ref, n1g_ref, w_ref, gq_ref, gk_ref, cos_ref, sin_ref,
                     qc_ref, ql_ref, kk_ref, vv_ref, kst_ref, vst_ref):
    lat = pl.program_id(0) >= N_CTX_TOK // TBP
    m = mods_ref[0]
    h = _modulate(x_ref[...], n1g_ref[...], m[:, 0:D], m[:, D:2 * D])
    qkv = _bdot(h.astype(BF16), w_ref[...])
    cos_t, sin_t = cos_ref[...], sin_ref[...]
    rope = lambda v: v * cos_t + pltpu.roll(v, 64, axis=1) * sin_t
    qs = GQA_SCALE * LOG2E
    for hd in range(GQA_HEADS):
        c0 = hd * 128
        qn = _rms_rows(qkv[:, c0:c0 + 128]) * gq_ref[...] * qs
        qc_ref[:, c0:c0 + 128] = qn.astype(BF16)
        ql_ref[:, c0:c0 + 128] = rope(qn).astype(BF16)
    kns = [_rms_rows(qkv[:, D + kh * 128:D + (kh + 1) * 128]) * gk_ref[...]
           for kh in range(GQA_KV_HEADS)]
    for kh in range(GQA_KV_HEADS):
        kk_ref[:, kh * 128:(kh + 1) * 128] = jnp.where(lat, rope(kns[kh]), kns[kh]).astype(BF16)
    v = qkv[:, D + 256:D + 512]
    vv_ref[...] = v.astype(BF16)

    @pl.when(jnp.logical_not(lat))
    def _state():
        for kh in range(GQA_KV_HEADS):
            kst_ref[:, kh * 128:(kh + 1) * 128] = kns[kh]
        vst_ref[...] = v


def _gqa_proj(x, mods, n1g, w_qkv, gq, gk, cos_t, sin_t):
    kvw = GQA_KV_HEADS * 128
    tw = functools.partial(_tok, tb=TBP)
    rope_spec = pl.BlockSpec((TBP, 128), lambda b: (_lat_blk(b, TBP), 0))
    return pl.pallas_call(
        _gqa_proj_kernel,
        out_shape=[jax.ShapeDtypeStruct((N_TOK, D), BF16),
                   jax.ShapeDtypeStruct((N_LAT_TOK, D), BF16),
                   jax.ShapeDtypeStruct((N_TOK, kvw), BF16),
                   jax.ShapeDtypeStruct((N_TOK, kvw), BF16),
                   jax.ShapeDtypeStruct((N_CTX_TOK, kvw), F32),
                   jax.ShapeDtypeStruct((N_CTX_TOK, kvw), F32)],
        grid=(N_TOK // TBP,),
        in_specs=[tw(D), _mods_spec(3, tb=TBP), _full((1, D)), _full((D, D + 2 * kvw)),
                  _full((1, 128)), _full((1, 128)), rope_spec, rope_spec],
        out_specs=[tw(D), _lat_only(D, TBP), tw(kvw), tw(kvw),
                   _ctx_only(kvw, TBP), _ctx_only(kvw, TBP)],
        compiler_params=_cparams(1),
        name="gqa_proj",
    )(x, mods, n1g, w_qkv, gq, gk, cos_t, sin_t)


REFINE_BITS = 15
F32_MIN_NORMAL = 2.0 ** -126


def _route_kernel(lg_ref, rel_ref, gate_ref, meta_ref, *, seg, n_units):
    n_tok = n_units * UNIT
    nseg = n_tok // seg
    n_chunks = n_tok // TB
    cap = seg // 8
    lg = lg_ref[...]
    ex = jnp.exp(lg - jnp.max(lg, axis=0, keepdims=True))
    aff = ex / jnp.sum(ex, axis=0, keepdims=True)
    affs = [aff[:, s * seg:(s + 1) * seg] for s in range(nseg)]
    col = lambda dtype: tuple(jnp.zeros((N_EXPERTS, 1), dtype) for _ in range(nseg))

    def count(mask):
        return jnp.sum(jnp.where(mask, 1.0, 0.0), axis=1, keepdims=True)

    def by_bits(i, bits):
        bit = jnp.left_shift(jnp.int32(1), 30 - i)
        out = []
        for s in range(nseg):
            cand = bits[s] | bit
            ok = count(affs[s] >= pltpu.bitcast(cand, F32)) >= cap
            out.append(jnp.where(ok, cand, bits[s]))
        return tuple(out)

    bits = lax.fori_loop(0, 31, by_bits, col(I32))
    encs = []
    for s in range(nseg):
        t = pltpu.bitcast(bits[s], F32)
        tn = pltpu.bitcast(bits[s] + 1, F32)
        inv = 1.0 / jnp.maximum(tn - t, F32_MIN_NORMAL)
        encs.append(jnp.where(affs[s] >= tn, 2.0, jnp.where(affs[s] >= t, (affs[s] - t) * inv, -1.0)))

    def refine(encs):
        def by_frac(i, cs):
            step = pltpu.bitcast(jnp.zeros((N_EXPERTS, 1), I32) + jnp.left_shift(126 - i, 23), F32)
            out = []
            for s in range(nseg):
                c_try = cs[s] + step
                out.append(jnp.where(count(encs[s] >= c_try) >= cap, c_try, cs[s]))
            return tuple(out)

        cs = lax.fori_loop(0, REFINE_BITS, by_frac, col(F32))
        w = 2.0 ** -REFINE_BITS
        return [jnp.where(e >= c + w, 2.0, jnp.where(e >= c, (e - c) * (1.0 / w), -1.0))
                for e, c in zip(encs, cs)]

    encs = refine(refine(encs))
    enc = jnp.concatenate(encs, axis=1) if nseg > 1 else encs[0]
    need = [cap - count(e >= 1.5) for e in encs]

    r = lax.broadcasted_iota(I32, (TB, TB), 0)
    c = lax.broadcasted_iota(I32, (TB, TB), 1)
    tri = jnp.where(r < c, 1.0, 0.0).astype(BF16)
    chunk = lambda v, cb: v[:, cb * TB:(cb + 1) * TB]
    stack = lambda masks: jnp.concatenate(
        [jnp.where(mk, 1.0, 0.0).astype(BF16) for mk in masks], axis=0)
    rows = lambda v, cb: v[cb * N_EXPERTS:(cb + 1) * N_EXPERTS]

    above = [chunk(enc, cb) >= 1.5 for cb in range(n_chunks)]
    tie = [(chunk(enc, cb) >= 0.0) & (chunk(enc, cb) < 1.5) for cb in range(n_chunks)]
    tie_rank = _bdot(stack(tie), tri)
    sel = []
    tie_carry = None
    for cb in range(n_chunks):
        if (cb * TB) % seg == 0:
            tie_carry = jnp.zeros((N_EXPERTS, 1), F32)
        keep = tie[cb] & (rows(tie_rank, cb) + tie_carry < need[(cb * TB) // seg])
        sel.append(above[cb] | keep)
        tie_carry = tie_carry + count(tie[cb])
    pos_local = _bdot(stack(sel), tri)

    lane = lax.broadcasted_iota(I32, (N_EXPERTS, 128), 1)
    lane8 = lax.broadcasted_iota(I32, (8, 128), 1)
    units_per_grp = GRP_BLK // UNIT_BLK
    for u in range(n_units):
        base = (u % units_per_grp) * (GRP_ROWS // units_per_grp)
        sel_carry = jnp.zeros((N_EXPERTS, 1), F32)
        meta_off = jnp.zeros((N_EXPERTS, 128), I32)
        meta_np = jnp.zeros((8, 128), I32)
        for j in range(UNIT_BLK):
            cb = u * UNIT_BLK + j
            pos = (rows(pos_local, cb) + sel_carry).astype(I32) + base
            off16 = (sel_carry.astype(I32) + base) & ~15
            sel_carry = sel_carry + count(sel[cb])
            rel = jnp.where(sel[cb], pos - off16, -1)
            rel_ref[:, cb * TB:(cb + 1) * TB] = rel
            gate_ref[:, cb * TB:(cb + 1) * TB] = jnp.where(sel[cb], chunk(aff, cb), 0.0)
            mx = jnp.max(jnp.max(rel, axis=1, keepdims=True), axis=0, keepdims=True)
            npass = jnp.zeros((1, 1), I32)
            for k in range(MAX_PASSES):
                npass = npass + jnp.where(mx >= k * WIN, 1, 0)
            meta_off = jnp.where(lane == j, off16, meta_off)
            meta_np = jnp.where(lane8 == j, npass, meta_np)
        meta_ref[u, 0:N_EXPERTS, :] = meta_off
        meta_ref[u, N_EXPERTS:N_EXPERTS + 8, :] = meta_np


def _route(logits, seg, name):
    n = logits.shape[1]
    n_units = n // UNIT
    return pl.pallas_call(
        functools.partial(_route_kernel, seg=seg, n_units=n_units),
        out_shape=[jax.ShapeDtypeStruct((N_EXPERTS, n), I32),
                   jax.ShapeDtypeStruct((N_EXPERTS, n), F32),
                   jax.ShapeDtypeStruct((n_units, N_EXPERTS + 8, 128), I32)],
        grid=(1,),
        in_specs=[pl.BlockSpec((N_EXPERTS, n), lambda i: (0, 0))],
        out_specs=[pl.BlockSpec((N_EXPERTS, n), lambda i: (0, 0)),
                   pl.BlockSpec((N_EXPERTS, n), lambda i: (0, 0)),
                   pl.BlockSpec((n_units, N_EXPERTS + 8, 128), lambda i: (0, 0, 0))],
        compiler_params=_cparams(1),
        name=name,
    )(logits)


def _moe_kernel(off_ref, np_ref, h2_ref, rel_ref, gt_ref, wg_ref, wu_ref, wd_ref, x1_ref,
                mods_ref, fg_ref, o_ref, xy_ref, *, final, g0):
    g = pl.program_id(0) + g0
    s = pl.program_id(1)

    def window_start(e, blk, p):
        return pl.multiple_of(off_ref[(g * N_EXPERTS + e) * GRP_BLK + blk] + p * WIN, 16)

    def passes(blk):
        return np_ref[g * GRP_BLK + blk]

    def one_hot(rel, p, gates):
        rows = lax.broadcasted_iota(I32, (WIN, TB), 0) + p * WIN
        parts = []
        for e in range(N_EXPERTS):
            hit = jnp.broadcast_to(rel[e:e + 1, :], (WIN, TB)) == rows
            val = 1.0 if gates is None else jnp.broadcast_to(gates[e:e + 1, :], (WIN, TB))
            parts.append(jnp.where(hit, val, 0.0).astype(BF16))
        return jnp.concatenate(parts, axis=0)

    @pl.when(s == 0)
    def _zero():
        xy_ref[...] = jnp.zeros(xy_ref.shape, BF16)

    @pl.when(s < MOE_TS)
    def _gather():
        for j in range(MOE_BPS):
            blk = s * MOE_BPS + j
            rel = rel_ref[:, j * TB:(j + 1) * TB]
            hb = h2_ref[j * TB:(j + 1) * TB, :]

            def one_pass(p):
                win = _bdot(one_hot(rel, p, None), hb).astype(BF16)
                for e in range(N_EXPERTS):
                    rows = pl.ds(window_start(e, blk, p), WIN)
                    xy_ref[e, rows, :] = xy_ref[e, rows, :] + win[e * WIN:(e + 1) * WIN]

            one_pass(0)

            def extra(p, carry):
                one_pass(p)
                return carry

            lax.fori_loop(1, passes(blk), extra, 0)

    @pl.when((s >= MOE_TS) & (s < MOE_TS + N_EXPERTS))
    def _ffn():
        e = s - MOE_TS
        x = xy_ref[e, pl.ds(0, GRP_ROWS), :]
        a = _bdot(x, wg_ref[0, 0])
        hid = (_silu(a) * _bdot(x, wu_ref[0, 0])).astype(BF16)
        xy_ref[e, pl.ds(0, GRP_ROWS), :] = _bdot(hid, wd_ref[0, 0]).astype(BF16)

    @pl.when(s >= MOE_TS + N_EXPERTS)
    def _combine():
        gate2 = mods_ref[0][:, 5 * D:6 * D]
        for j in range(MOE_BPS):
            blk = (s - (MOE_TS + N_EXPERTS)) * MOE_BPS + j
            rel = rel_ref[:, j * TB:(j + 1) * TB]
            gates = gt_ref[:, j * TB:(j + 1) * TB]

            def one_pass(p):
                wins = [xy_ref[e, pl.ds(window_start(e, blk, p), WIN), :] for e in range(N_EXPERTS)]
                return lax.dot_general(one_hot(rel, p, gates), jnp.concatenate(wins, axis=0), TN,
                                       preferred_element_type=F32)

            y = lax.fori_loop(1, passes(blk), lambda p, acc: acc + one_pass(p), one_pass(0))
            out = x1_ref[j * TB:(j + 1) * TB, :] + gate2 * y
            if final:
                out = _rms_rows(out) * fg_ref[...]
            o_ref[j * TB:(j + 1) * TB, :] = out


def _moe_layer(layer, off16, npass, h2, rel, gates, wg, wu, wd, x1, mods, final_g, final,
               g0=0, n_grp=N_GRP):
    n_steps = 2 * MOE_TS + N_EXPERTS
    tbs = MOE_BPS * TB

    def gather_blk(g, s, *_):
        return (g0 + g) * MOE_TS + jnp.minimum(s, MOE_TS - 1)

    def route_blk(g, s, *_):
        return (g0 + g) * MOE_TS + jnp.where(s < MOE_TS + N_EXPERTS, jnp.minimum(s, MOE_TS - 1),
                                             s - (MOE_TS + N_EXPERTS))

    def combine_blk(g, s, *_):
        return (g0 + g) * MOE_TS + jnp.maximum(s - (MOE_TS + N_EXPERTS), 0)

    def expert(g, s, *_):
        return jnp.clip(s - MOE_TS, 0, N_EXPERTS - 1)

    grid_spec = pltpu.PrefetchScalarGridSpec(
        num_scalar_prefetch=2,
        grid=(n_grp, n_steps),
        in_specs=[
            pl.BlockSpec((tbs, D), lambda *a: (gather_blk(*a), 0)),
            pl.BlockSpec((N_EXPERTS, tbs), lambda *a: (0, route_blk(*a))),
            pl.BlockSpec((N_EXPERTS, tbs), lambda *a: (0, route_blk(*a))),
            pl.BlockSpec((1, 1, D, EXPERT_FF), lambda *a: (layer, expert(*a), 0, 0)),
            pl.BlockSpec((1, 1, D, EXPERT_FF), lambda *a: (layer, expert(*a), 0, 0)),
            pl.BlockSpec((1, 1, EXPERT_FF, D), lambda *a: (layer, expert(*a), 0, 0)),
            pl.BlockSpec((tbs, D), lambda *a: (combine_blk(*a), 0)),
            pl.BlockSpec((1, 1, 6 * D),
                         lambda *a: (layer * N_COND + _cond_row(combine_blk(*a), tbs), 0, 0)),
            pl.BlockSpec((1, D), lambda *a: (0, 0)),
        ],
        out_specs=pl.BlockSpec((tbs, D), lambda *a: (combine_blk(*a) - g0 * MOE_TS, 0)),
        scratch_shapes=[pltpu.VMEM((N_EXPERTS, XY_ROWS, D), BF16)],
    )
    return pl.pallas_call(
        functools.partial(_moe_kernel, final=final, g0=g0),
        out_shape=jax.ShapeDtypeStruct((n_grp * GRP_BLK * TB, D), F32),
        grid_spec=grid_spec,
        compiler_params=_cparams(2),
        name="moe_experts",
    )(off16, npass, h2, rel, gates, wg, wu, wd, x1, mods, final_g)


def _moe(layer, x1, h2, logits, wg, wu, wd, mods, final_g, final):
    rel_c, gate_c, meta_c = _route(logits[:, :N_CTX_TOK], CTX_LEN, "route_ctx")
    rel_l, gate_l, meta_l = _route(logits[:, N_CTX_TOK:], LAT_LEN, "route_lat")
    rel = jnp.concatenate([rel_c, rel_l], axis=1)
    gates = jnp.concatenate([gate_c, gate_l], axis=1)
    meta = jnp.concatenate([meta_c, meta_l], axis=0)
    upg = GRP_BLK // UNIT_BLK
    off16 = meta[:, :N_EXPERTS, :UNIT_BLK].reshape(N_GRP, upg, N_EXPERTS, UNIT_BLK)
    off16 = jnp.swapaxes(off16, 1, 2).reshape(-1)
    npass = meta[:, N_EXPERTS, :UNIT_BLK].reshape(-1)
    args = (layer, off16, npass, h2, rel, gates, wg, wu, wd, x1, mods, final_g, final)
    if not final:
        return _moe_layer(*args)
    ctx_grp = N_CTX_BLK // GRP_BLK
    return _moe_layer(*args, g0=0, n_grp=ctx_grp), _moe_layer(*args, g0=ctx_grp, n_grp=N_GRP - ctx_grp)


def _axial_tables(rot_dim):
    t = jnp.arange(LAT_LEN)
    row = (t // GRID_W).astype(F32)
    col = (t % GRID_W).astype(F32)
    axis_dim = rot_dim // 2
    inv = ROPE_THETA ** (-jnp.arange(0, axis_dim, 2, dtype=F32) / axis_dim)
    ang = jnp.concatenate([row[:, None] * inv, col[:, None] * inv], axis=-1)
    return jnp.cos(ang), jnp.sin(ang)


def kernel(x_prompt, x_sample, c, cache_mla_ckv, cache_mla_krope, cache_gqa_k, cache_gqa_v, c_ctx, ada_w, ada_b, norm1_g, norm2_g, final_g, pool_w, pool_scale, mla_w_down, mla_g_q, mla_w_uq, mla_g_kv, mla_w_ukv, mla_w_o, conv_w_pw1, conv_b_pw1, conv_w_dw, conv_b_dw, conv_ln_g, conv_ln_b, conv_w_pw2, conv_b_pw2, gqa_w_qkv, gqa_g_q, gqa_g_k, gqa_w_o, router_w, moe_w_gate, moe_w_up, moe_w_down):
    cond =jnp.zeros((N_COND, D), F32).at[0].set(c_ctx).at[1:1 + N_LAT_SEQ].set(c)
    mods = _adaln(cond, ada_w, ada_b).reshape(DEPTH * N_COND, 1, 6 * D)

    row = lambda v: v.reshape(1, -1)
    rw_t = jnp.swapaxes(router_w, 1, 2)
    rwh = rw_t.astype(BF16)
    rw2 = jnp.concatenate([rwh, (rw_t - rwh.astype(F32)).astype(BF16)], axis=1)
    wg, wu, wd = moe_w_gate.astype(BF16), moe_w_up.astype(BF16), moe_w_down.astype(BF16)
    fg = row(final_g)

    def post_args(l):
        return row(norm2_g[l]), rw2[l]

    def moe(l, x1, h2, lg):
        return _moe(l, x1, h2, lg, wg, wu, wd, mods, fg, l == DEPTH - 1)

    x1, h2, lg = _pool_layer(x_prompt.reshape(N_CTX_TOK, D), x_sample.reshape(N_LAT_TOK, D), mods,
                             row(norm1_g[0]), pool_w[0].astype(BF16), row(pool_scale[0]),
                             *post_args(0))
    x = moe(0, x1, h2, lg)

    cos, sin = _axial_tables(MLA_ROPE)
    z = jnp.zeros((LAT_LEN, 64), F32)
    cos_t = jnp.concatenate([cos, cos, z], axis=1)
    sin_t = jnp.concatenate([-sin, sin, z], axis=1)
    w_down = jnp.pad(mla_w_down[0], ((0, 0), (0, MLA_DOWN_PAD - mla_w_down.shape[2]))).astype(BF16)
    w_uq = jnp.pad(mla_w_uq[0].reshape(MLA_Q_LORA, MLA_HEADS, MLA_NOPE + MLA_ROPE),
                   ((0, 0), (0, 0), (0, MLA_QK - MLA_NOPE - MLA_ROPE)))
    w_uq = w_uq.reshape(MLA_Q_LORA, MLA_HEADS * MLA_QK).astype(BF16)
    w_ukv = mla_w_ukv[0].astype(BF16)
    qc, ql, kmat, vmat, ckv_n, krp = _mla_proj(
        x, mods, row(norm1_g[1]), w_down, row(mla_g_q[0]), w_uq, row(mla_g_kv[0]), w_ukv, cos_t, sin_t)
    cache_kr = jnp.pad(cache_mla_krope[:, 0].reshape(N_LAT_SEQ * PAST_LEN, MLA_ROPE),
                       ((0, 0), (0, 128 - MLA_ROPE)))
    kc, vc = _mla_cache(cache_mla_ckv[:, 0].reshape(N_LAT_SEQ * PAST_LEN, MLA_KV_LORA), cache_kr, w_ukv)
    a_ctx = _attn_ctx(qc, kmat, vmat, MLA_HEADS, MLA_HEADS, MLA_QK, "mla_attn_ctx")
    a_lat = _attn_lat(qc, ql, kc, kmat, vc, vmat, MLA_HEADS, MLA_HEADS, MLA_QK, "mla_attn_lat")
    x1, h2, lg = _oproj_layer(1, a_ctx, a_lat, mla_w_o[0].astype(BF16), x, mods, *post_args(1))
    x = moe(1, x1, h2, lg)
    state_ckv = ckv_n.reshape(N_CTX_SEQ, 1, CTX_LEN, MLA_KV_LORA)
    state_kr = krp.reshape(N_CTX_SEQ, 1, CTX_LEN, MLA_ROPE)

    u = _conv_in(x, mods, row(norm1_g[2]), conv_w_pw1[0].astype(BF16), row(conv_b_pw1[0]))
    x1, h2, lg = _conv_out(u, conv_w_dw[0], row(conv_b_dw[0]), row(conv_ln_g[0]), row(conv_ln_b[0]),
                           conv_w_pw2[0].astype(BF16), row(conv_b_pw2[0]), x, mods, *post_args(2))
    x = moe(2, x1, h2, lg)

    cos, sin = _axial_tables(GQA_HEAD_DIM)
    cos_t = jnp.concatenate([cos, cos], axis=1)
    sin_t = jnp.concatenate([-sin, sin], axis=1)
    qc, ql, kk, vv, kst, vst = _gqa_proj(x, mods, row(norm1_g[3]), gqa_w_qkv[0].astype(BF16),
                                         row(gqa_g_q[0]), row(gqa_g_k[0]), cos_t, sin_t)
    kvw = GQA_KV_HEADS * GQA_HEAD_DIM
    kc = cache_gqa_k[:, 0].reshape(N_LAT_SEQ * PAST_LEN, kvw).astype(BF16)
    vc = cache_gqa_v[:, 0].reshape(N_LAT_SEQ * PAST_LEN, kvw).astype(BF16)
    a_ctx = _attn_ctx(qc, kk, vv, GQA_HEADS, GQA_KV_HEADS, GQA_HEAD_DIM, "gqa_attn_ctx")
    a_lat = _attn_lat(qc, ql, kc, kk, vc, vv, GQA_HEADS, GQA_KV_HEADS, GQA_HEAD_DIM, "gqa_attn_lat")
    x1, h2, lg = _oproj_layer(3, a_ctx, a_lat, gqa_w_o[0].astype(BF16), x, mods, *post_args(3))
    y_ctx, y_lat = moe(3, x1, h2, lg)
    state_k = kst.reshape(N_CTX_SEQ, 1, CTX_LEN, GQA_KV_HEADS, GQA_HEAD_DIM)
    state_v = vst.reshape(N_CTX_SEQ, 1, CTX_LEN, GQA_KV_HEADS, GQA_HEAD_DIM)

    y_prompt = y_ctx.reshape(N_CTX_SEQ, CTX_LEN, D)
    y_sample = y_lat.reshape(N_LAT_SEQ, LAT_LEN, D)
    return (y_prompt, y_sample, state_ckv, state_kr, state_k, state_v)
```

```python
import functools
import math

import jax
import jax.numpy as jnp
from jax import lax
from jax.experimental import pallas as pl
from jax.experimental.pallas import tpu as pltpu

F32 = jnp.float32
BF16 = jnp.bfloat16
I32 = jnp.int32

D = 1024
DEPTH = 4
EPS = 1e-6
GRID_W = 64
ROPE_THETA = 10000.0
LOG2E = math.log2(math.e)

TB = 256
TBW = 512
TBP = 256
HALO = 16
N_CTX_SEQ, CTX_LEN = 32, 256
N_LAT_SEQ, LAT_LEN = 8, 2048
PAST_LEN = 512
N_CTX_TOK = N_CTX_SEQ * CTX_LEN
N_LAT_TOK = N_LAT_SEQ * LAT_LEN
N_TOK = N_CTX_TOK + N_LAT_TOK
N_CTX_BLK = N_CTX_TOK // TB
LAT_BLK = LAT_LEN // TB
N_BLK = N_TOK // TB
N_COND = 16

POOL_WINDOWS = (2, 4, 8, 16)
POOL_GW = 256
MLA_HEADS = 8
MLA_Q_LORA = 384
MLA_KV_LORA = 256
MLA_NOPE = 128
MLA_ROPE = 64
MLA_DOWN_PAD = 768
MLA_QK = 256
MLA_SCALE = (MLA_NOPE + MLA_ROPE) ** -0.5
CONV_WIDTH = 31
GQA_HEADS = 8
GQA_KV_HEADS = 2
GQA_HEAD_DIM = 128
GQA_SCALE = GQA_HEAD_DIM ** -0.5
N_EXPERTS = 16
EXPERT_FF = 512

UNIT = 2048
UNIT_BLK = UNIT // TB
GRP_BLK = 16
N_GRP = N_BLK // GRP_BLK
GRP_ROWS = 512
WIN = 64
MAX_PASSES = -(-(TB + 15) // WIN)
CTX_WIN = CTX_LEN // 8
XY_ROWS = GRP_ROWS + WIN
MOE_BPS = 2
MOE_TS = GRP_BLK // MOE_BPS

NT = (((1,), (1,)), ((), ()))
TN = (((0,), (0,)), ((), ()))

VMEM_LIMIT = 56 * 1024 * 1024


def _cparams(n_axes):
    return pltpu.CompilerParams(dimension_semantics=("arbitrary",) * n_axes,
                                vmem_limit_bytes=VMEM_LIMIT)


def _cond_row(b, tb=TB):
    n_ctx = N_CTX_TOK // tb
    return jnp.where(b < n_ctx, 0, 1 + (b - n_ctx) // (LAT_LEN // tb))


def _lat_blk(b, tb=TB):
    n_ctx = N_CTX_TOK // tb
    return jnp.where(b < n_ctx, 0, (b - n_ctx) % (LAT_LEN // tb))


def _mods_spec(layer, blk_of=lambda *g: g[0], tb=TB):
    return pl.BlockSpec((1, 1, 6 * D),
                        lambda *g: (layer * N_COND + _cond_row(blk_of(*g), tb), 0, 0))


def _full(shape):
    return pl.BlockSpec(shape, lambda *g: (0,) * len(shape))


def _tok(width, blk_of=lambda *g: g[0], tb=TB):
    return pl.BlockSpec((tb, width), lambda *g: (blk_of(*g), 0))


def _ctx_only(width, tb=TB):
    return pl.BlockSpec((tb, width), lambda b: (jnp.minimum(b, N_CTX_TOK // tb - 1), 0))


def _lat_only(width, tb=TB):
    return pl.BlockSpec((tb, width), lambda b: (jnp.maximum(b - N_CTX_TOK // tb, 0), 0))


def _halo_specs(width):
    per = TB // HALO
    prev = pl.BlockSpec((HALO, width), lambda b: (jnp.maximum(b * per - 1, 0), 0))
    nxt = pl.BlockSpec((HALO, width), lambda b: (jnp.minimum((b + 1) * per, N_TOK // HALO - 1), 0))
    return prev, nxt


def _seq_edges(b):
    lat = b >= N_CTX_BLK
    j = _lat_blk(b)
    return lat & (j > 0), lat & (j < LAT_BLK - 1)


def _rms_rows(x):
    return x * lax.rsqrt(jnp.mean(x * x, axis=-1, keepdims=True) + EPS)


def _modulate(x, g, shift, scale):
    return _rms_rows(x) * g * (1.0 + scale) + shift


def _silu(x):
    return x * jax.nn.sigmoid(x)


def _bdot(a, b):
    return jnp.dot(a, b, preferred_element_type=F32)


def _adaln_kernel(c_ref, w_ref, b_ref, o_ref):
    s = _silu(c_ref[...]).astype(BF16)
    o_ref[0] = _bdot(s, w_ref[0].astype(BF16)) + b_ref[0]


def _adaln(cond, ada_w, ada_b):
    tn = 1536
    return pl.pallas_call(
        _adaln_kernel,
        out_shape=jax.ShapeDtypeStruct((DEPTH, N_COND, 6 * D), F32),
        grid=(DEPTH, 6 * D // tn),
        in_specs=[
            _full((N_COND, D)),
            pl.BlockSpec((1, D, tn), lambda l, n: (l, 0, n)),
            pl.BlockSpec((1, 1, tn), lambda l, n: (l, 0, n)),
        ],
        out_specs=pl.BlockSpec((1, N_COND, tn), lambda l, n: (l, 0, n)),
        compiler_params=_cparams(2),
        name="adaln",
    )(cond, ada_w, ada_b.reshape(DEPTH, 1, 6 * D))


def _post(x, y, m, n2g_ref, rw_ref, x1_ref, h2_ref, lg_ref, rows=slice(None)):
    x1 = x + m[:, 2 * D:3 * D] * y
    x1_ref[rows, :] = x1
    h2 = _modulate(x1, n2g_ref[...], m[:, 3 * D:4 * D], m[:, 4 * D:5 * D])
    hh = h2.astype(BF16)
    hl = (h2 - hh.astype(F32)).astype(BF16)
    h2_ref[rows, :] = hh
    n = x.shape[0]
    r = lax.dot_general(rw_ref[...], jnp.concatenate([hh, hl], axis=0), NT,
                        preferred_element_type=F32)
    lg_ref[:, rows] = r[:N_EXPERTS, :n] + r[N_EXPERTS:, :n] + r[:N_EXPERTS, n:]


_POST_IN = [_full((1, D)), _full((2 * N_EXPERTS, D))]
_POST_OUT_SHAPES = [jax.ShapeDtypeStruct((N_TOK, D), F32),
                    jax.ShapeDtypeStruct((N_TOK, D), BF16),
                    jax.ShapeDtypeStruct((N_EXPERTS, N_TOK), F32)]


def _post_out_specs(tb=TB):
    return [_tok(D, tb=tb), _tok(D, tb=tb), pl.BlockSpec((N_EXPERTS, tb), lambda b: (0, b))]


def _pool_kernel(xc_ref, xl_ref, xp_ref, xn_ref, mods_ref, n1g_ref, pw_ref, ps_ref,
                 n2g_ref, rw_ref, x1_ref, h2_ref, lg_ref):
    b = pl.program_id(0)
    has_prev, has_next = _seq_edges(b)
    m = mods_ref[0]
    g, sh, sc = n1g_ref[...], m[:, 0:D], m[:, D:2 * D]
    x = jnp.where(b >= N_CTX_BLK, xl_ref[...], xc_ref[...])
    hc = _modulate(x, g, sh, sc)
    hext = jnp.concatenate([_modulate(xp_ref[...], g, sh, sc), hc,
                            _modulate(xn_ref[...], g, sh, sc)], axis=0)
    ext = TB + 2 * HALO
    t = lax.broadcasted_iota(I32, (TB, ext), 0)
    u = lax.broadcasted_iota(I32, (TB, ext), 1) - HALO
    valid = ((u >= 0) | has_prev) & ((u < TB) | has_next)
    tc = lax.broadcasted_iota(I32, (TB, 1), 0)
    lo_lim = jnp.where(has_prev, -HALO, 0)
    hi_lim = jnp.where(has_next, TB + HALO, TB)
    ys = []
    for gi, win in enumerate(POOL_WINDOWS):
        hw = win // 2
        band = jnp.where((u >= t - hw) & (u < t + hw) & valid, 1.0, 0.0).astype(BF16)
        cnt = (jnp.minimum(tc + hw, hi_lim) - jnp.maximum(tc - hw, lo_lim)).astype(F32)
        hg = hext[:, gi * POOL_GW:(gi + 1) * POOL_GW]
        hi = hg.astype(BF16)
        lo = (hg - hi.astype(F32)).astype(BF16)
        tot = _bdot(band, hi) + _bdot(band, lo)
        pooled = tot / cnt - hc[:, gi * POOL_GW:(gi + 1) * POOL_GW]
        ys.append(_bdot(pooled.astype(BF16), pw_ref[gi]))
    y = jnp.concatenate(ys, axis=1) * ps_ref[...]
    _post(x, y, m, n2g_ref, rw_ref, x1_ref, h2_ref, lg_ref)


def _pool_layer(x_ctx, x_lat, mods, n1g, pool_w, pool_scale, n2g, rw):
    per = TB // HALO
    lat_b = lambda b: jnp.maximum(b - N_CTX_BLK, 0)
    prev = pl.BlockSpec((HALO, D), lambda b: (jnp.maximum(lat_b(b) * per - 1, 0), 0))
    nxt = pl.BlockSpec((HALO, D),
                       lambda b: (jnp.minimum((lat_b(b) + 1) * per, N_LAT_TOK // HALO - 1), 0))
    return pl.pallas_call(
        _pool_kernel,
        out_shape=_POST_OUT_SHAPES,
        grid=(N_BLK,),
        in_specs=[_ctx_only(D), _lat_only(D), prev, nxt, _mods_spec(0), _full((1, D)),
                  _full((4, POOL_GW, POOL_GW)), _full((1, D))] + _POST_IN,
        out_specs=_post_out_specs(),
        compiler_params=_cparams(1),
        name="pool_mixer",
    )(x_ctx, x_lat, x_lat, x_lat, mods, n1g, pool_w, pool_scale, n2g, rw)


def _rope_half32(v, cos_t, sin_t):
    lane = lax.broadcasted_iota(I32, v.shape, 1)
    swapped = jnp.where(lane < 32, pltpu.roll(v, 96, axis=1), pltpu.roll(v, 32, axis=1))
    return v * cos_t + swapped * sin_t


def _mla_proj_kernel(x_ref, mods_ref, n1g_ref, wd_ref, gq_ref, wuq_ref, gkv_ref, wukv_ref,
                     cos_ref, sin_ref, qc_ref, ql_ref, k_ref, v_ref, ckv_ref, kr_ref):
    lat = pl.program_id(0) >= N_CTX_TOK // TBP
    m = mods_ref[0]
    h = _modulate(x_ref[...], n1g_ref[...], m[:, 0:D], m[:, D:2 * D])
    d = _bdot(h.astype(BF16), wd_ref[...])
    cq = d[:, :MLA_Q_LORA]
    ckv = d[:, MLA_Q_LORA:MLA_Q_LORA + MLA_KV_LORA]
    krp = d[:, MLA_Q_LORA + MLA_KV_LORA:]
    q = _bdot((_rms_rows(cq) * gq_ref[...]).astype(BF16), wuq_ref[...])
    ckv_n = _rms_rows(ckv) * gkv_ref[...]
    kv = _bdot(ckv_n.astype(BF16), wukv_ref[...])
    cos_t, sin_t = cos_ref[...], sin_ref[...]
    k_rope = jnp.where(lat, _rope_half32(krp, cos_t, sin_t), krp).astype(BF16)
    qs = MLA_SCALE * LOG2E
    for hd in range(MLA_HEADS):
        c0 = hd * MLA_QK
        qn = (q[:, c0:c0 + 128] * qs).astype(BF16)
        qr = q[:, c0 + 128:c0 + 256] * qs
        qc_ref[:, c0:c0 + 128] = qn
        qc_ref[:, c0 + 128:c0 + 256] = qr.astype(BF16)
        ql_ref[:, c0:c0 + 128] = qn
        ql_ref[:, c0 + 128:c0 + 256] = _rope_half32(qr, cos_t, sin_t).astype(BF16)
        k_ref[:, c0:c0 + 128] = kv[:, c0:c0 + 128].astype(BF16)
        k_ref[:, c0 + 128:c0 + 256] = k_rope
        v_ref[:, hd * 128:(hd + 1) * 128] = kv[:, c0 + 128:c0 + 256].astype(BF16)

    @pl.when(jnp.logical_not(lat))
    def _state():
        ckv_ref[...] = ckv_n
        kr_ref[...] = krp[:, :MLA_ROPE]


def _mla_proj(x, mods, n1g, wd, gq, wuq, gkv, wukv, cos_t, sin_t):
    hq = MLA_HEADS * MLA_QK
    tw = functools.partial(_tok, tb=TBP)
    rope_spec = pl.BlockSpec((TBP, 128), lambda b: (_lat_blk(b, TBP), 0))
    return pl.pallas_call(
        _mla_proj_kernel,
        out_shape=[jax.ShapeDtypeStruct((N_TOK, hq), BF16),
                   jax.ShapeDtypeStruct((N_LAT_TOK, hq), BF16),
                   jax.ShapeDtypeStruct((N_TOK, hq), BF16),
                   jax.ShapeDtypeStruct((N_TOK, MLA_HEADS * 128), BF16),
                   jax.ShapeDtypeStruct((N_CTX_TOK, MLA_KV_LORA), F32),
                   jax.ShapeDtypeStruct((N_CTX_TOK, MLA_ROPE), F32)],
        grid=(N_TOK // TBP,),
        in_specs=[tw(D), _mods_spec(1, tb=TBP), _full((1, D)), _full((D, MLA_DOWN_PAD)),
                  _full((1, MLA_Q_LORA)), _full((MLA_Q_LORA, hq)), _full((1, MLA_KV_LORA)),
                  _full((MLA_KV_LORA, hq)), rope_spec, rope_spec],
        out_specs=[tw(hq), _lat_only(hq, TBP), tw(hq), tw(MLA_HEADS * 128),
                   _ctx_only(MLA_KV_LORA, TBP), _ctx_only(MLA_ROPE, TBP)],
        compiler_params=_cparams(1),
        name="mla_proj",
    )(x, mods, n1g, wd, gq, wuq, gkv, wukv, cos_t, sin_t)


def _mla_cache_kernel(ckv_ref, kr_ref, wukv_ref, k_ref, v_ref):
    kv = _bdot(ckv_ref[...].astype(BF16), wukv_ref[...])
    kr = kr_ref[...].astype(BF16)
    for hd in range(MLA_HEADS):
        c0 = hd * MLA_QK
        k_ref[:, c0:c0 + 128] = kv[:, c0:c0 + 128].astype(BF16)
        k_ref[:, c0 + 128:c0 + 256] = kr
        v_ref[:, hd * 128:(hd + 1) * 128] = kv[:, c0 + 128:c0 + 256].astype(BF16)


def _mla_cache(ckv, krp, wukv):
    n = ckv.shape[0]
    hq = MLA_HEADS * MLA_QK
    return pl.pallas_call(
        _mla_cache_kernel,
        out_shape=[jax.ShapeDtypeStruct((n, hq), BF16),
                   jax.ShapeDtypeStruct((n, MLA_HEADS * 128), BF16)],
        grid=(n // TB,),
        in_specs=[_tok(MLA_KV_LORA), _tok(128), _full((MLA_KV_LORA, hq))],
        out_specs=[_tok(hq), _tok(MLA_HEADS * 128)],
        compiler_params=_cparams(1),
        name="mla_cache_kv",
    )(ckv, krp, wukv)


def _attn_ctx_kernel(q_ref, k_ref, v_ref, o_ref, *, heads, kv_heads, dk):
    rep = heads // kv_heads
    for hd in range(heads):
        kh = hd // rep
        q = q_ref[:, hd * dk:(hd + 1) * dk]
        k = k_ref[:, kh * dk:(kh + 1) * dk]
        s = lax.dot_general(q, k, NT, preferred_element_type=F32)
        p = jnp.exp2(s - jnp.max(s, axis=-1, keepdims=True))
        l = jnp.sum(p, axis=-1, keepdims=True)
        o = _bdot(p.astype(BF16), v_ref[:, kh * 128:(kh + 1) * 128])
        o_ref[:, hd * 128:(hd + 1) * 128] = (o / l).astype(BF16)


def _attn_ctx(q, k, v, heads, kv_heads, dk, name):
    return pl.pallas_call(
        functools.partial(_attn_ctx_kernel, heads=heads, kv_heads=kv_heads, dk=dk),
        out_shape=jax.ShapeDtypeStruct((N_CTX_TOK, heads * 128), BF16),
        grid=(N_CTX_SEQ,),
        in_specs=[_tok(heads * dk), _tok(kv_heads * dk), _tok(kv_heads * 128)],
        out_specs=_tok(heads * 128),
        compiler_params=_cparams(1),
        name=name,
    )(q, k, v)


ATT_TQ = 2048


ATT_SUB = 256


def _attn_lat_kernel(qc_ref, ql_ref, kc_ref, kl_ref, vc_ref, vl_ref, o_ref):
    for i in range(ATT_TQ // ATT_SUB):
        rows = pl.ds(i * ATT_SUB, ATT_SUB)
        sc = lax.dot_general(qc_ref[rows, :], kc_ref[...], NT, preferred_element_type=F32)
        sl = lax.dot_general(ql_ref[rows, :], kl_ref[...], NT, preferred_element_type=F32)
        mx = jnp.maximum(jnp.max(sc, axis=-1, keepdims=True), jnp.max(sl, axis=-1, keepdims=True))
        pc = jnp.exp2(sc - mx)
        pn = jnp.exp2(sl - mx)
        l = jnp.sum(pc, axis=-1, keepdims=True) + jnp.sum(pn, axis=-1, keepdims=True)
        o = _bdot(pc.astype(BF16), vc_ref[...]) + _bdot(pn.astype(BF16), vl_ref[...])
        o_ref[rows, :] = (o / l).astype(BF16)


def _attn_lat(qc, ql, kc, kl, vc, vl, heads, kv_heads, dk, name):
    rep = heads // kv_heads
    nq = LAT_LEN // ATT_TQ
    ctx_q_blk = N_CTX_TOK // ATT_TQ
    ctx_kv_blk = N_CTX_TOK // LAT_LEN
    return pl.pallas_call(
        _attn_lat_kernel,
        out_shape=jax.ShapeDtypeStruct((N_LAT_TOK, heads * 128), BF16),
        grid=(N_LAT_SEQ, heads, nq),
        in_specs=[
            pl.BlockSpec((ATT_TQ, dk), lambda b, h, i: (ctx_q_blk + b * nq + i, h)),
            pl.BlockSpec((ATT_TQ, dk), lambda b, h, i: (b * nq + i, h)),
            pl.BlockSpec((PAST_LEN, dk), lambda b, h, i: (b, h // rep)),
            pl.BlockSpec((LAT_LEN, dk), lambda b, h, i: (ctx_kv_blk + b, h // rep)),
            pl.BlockSpec((PAST_LEN, 128), lambda b, h, i: (b, h // rep)),
            pl.BlockSpec((LAT_LEN, 128), lambda b, h, i: (ctx_kv_blk + b, h // rep)),
        ],
        out_specs=pl.BlockSpec((ATT_TQ, 128), lambda b, h, i: (b * nq + i, h)),
        compiler_params=_cparams(3),
        name=name,
    )(qc, ql, kc, kl, vc, vl)


def _oproj_kernel(ac_ref, al_ref, w_ref, x_ref, mods_ref,
                  n2g_ref, rw_ref, x1_ref, h2_ref, lg_ref):
    lat = pl.program_id(0) >= N_CTX_TOK // TBW
    a = jnp.where(lat, al_ref[...], ac_ref[...])
    y = _bdot(a, w_ref[...])
    _post(x_ref[...], y, mods_ref[0], n2g_ref, rw_ref, x1_ref, h2_ref, lg_ref)


def _oproj_layer(layer, a_ctx, a_lat, w_o, x, mods, n2g, rw):
    return pl.pallas_call(
        _oproj_kernel,
        out_shape=_POST_OUT_SHAPES,
        grid=(N_TOK // TBW,),
        in_specs=[_ctx_only(D, TBW), _lat_only(D, TBW), _full((D, D)), _tok(D, tb=TBW),
                  _mods_spec(layer, tb=TBW)] + _POST_IN,
        out_specs=_post_out_specs(TBW),
        compiler_params=_cparams(1),
        name="attn_out_proj",
    )(a_ctx, a_lat, w_o, x, mods, n2g, rw)


def _conv_kernel(x_ref, xp_ref, xn_ref, mods_ref, n1g_ref, w1_ref, b1_ref, wdw_ref, bdw_ref,
                 lng_ref, lnb_ref, w2_ref, b2_ref, n2g_ref, rw_ref, x1_ref, h2_ref, lg_ref):
    b = pl.program_id(0)
    has_prev, has_next = _seq_edges(b)
    m = mods_ref[0]
    x = x_ref[...]
    xcat = jnp.concatenate([xp_ref[...], x, xn_ref[...]], axis=0)
    h = _modulate(xcat, n1g_ref[...], m[:, 0:D], m[:, D:2 * D]).astype(BF16)
    ext = TB + 2 * HALO
    row = lax.broadcasted_iota(I32, (ext, 1), 0)
    inside = ((row >= HALO) | has_prev) & ((row < HALO + TB) | has_next)
    wdw = wdw_ref[...]
    half = CONV_WIDTH // 2
    cols = []
    for p0 in range(0, D, 256):
        val = _bdot(h, w1_ref[:, p0:p0 + 256]) + b1_ref[:, p0:p0 + 256]
        gate = _bdot(h, w1_ref[:, D + p0:D + p0 + 256]) + b1_ref[:, D + p0:D + p0 + 256]
        ucat = jnp.where(inside, val * jax.nn.sigmoid(gate), 0.0)
        shifted = [ucat if r == 0 else pltpu.roll(ucat, ext - r, axis=0) for r in range(8)]
        for c0 in range(0, 256, 128):
            acc = jnp.zeros((TB, 128), F32)
            for k in range(CONV_WIDTH):
                a, r = divmod(k + HALO - half, 8)
                acc = acc + (shifted[r][8 * a:8 * a + TB, c0:c0 + 128]
                             * wdw[k:k + 1, p0 + c0:p0 + c0 + 128])
            cols.append(acc)
    u = jnp.concatenate(cols, axis=1) + bdw_ref[...]
    mu = jnp.mean(u, axis=-1, keepdims=True)
    uc = u - mu
    var = jnp.mean(uc * uc, axis=-1, keepdims=True)
    un_ = uc * lax.rsqrt(var + EPS) * lng_ref[...] + lnb_ref[...]
    y = _bdot(_silu(un_).astype(BF16), w2_ref[...]) + b2_ref[...]
    _post(x, y, m, n2g_ref, rw_ref, x1_ref, h2_ref, lg_ref)


def _conv_layer(x, mods, n1g, w_pw1, b_pw1, w_dw, b_dw, ln_g, ln_b, w_pw2, b_pw2, n2g, rw):
    prev, nxt = _halo_specs(D)
    return pl.pallas_call(
        _conv_kernel,
        out_shape=_POST_OUT_SHAPES,
        grid=(N_BLK,),
        in_specs=[_tok(D), prev, nxt, _mods_spec(2), _full((1, D)), _full((D, 2 * D)),
                  _full((1, 2 * D)), _full((CONV_WIDTH, D)), _full((1, D)), _full((1, D)),
                  _full((1, D)), _full((D, D)), _full((1, D))] + _POST_IN,
        out_specs=_post_out_specs(),
        compiler_params=_cparams(1),
        name="conv_module",
    )(x, x, x, mods, n1g, w_pw1, b_pw1, w_dw, b_dw, ln_g, ln_b, w_pw2, b_pw2, n2g, rw)


def _gqa_proj_kernel(x_ref, mods_ref, n1g_ref, w_ref, gq_ref, gk_ref, cos_ref, sin_ref,
                     qc_ref, ql_ref, kk_ref, vv_ref, kst_ref, vst_ref):
    lat = pl.program_id(0) >= N_CTX_TOK // TBP
    m = mods_ref[0]
    h = _modulate(x_ref[...], n1g_ref[...], m[:, 0:D], m[:, D:2 * D])
    qkv = _bdot(h.astype(BF16), w_ref[...])
    cos_t, sin_t = cos_ref[...], sin_ref[...]
    rope = lambda v: v * cos_t + pltpu.roll(v, 64, axis=1) * sin_t
    qs = GQA_SCALE * LOG2E
    for hd in range(GQA_HEADS):
        c0 = hd * 128
        qn = _rms_rows(qkv[:, c0:c0 + 128]) * gq_ref[...] * qs
        qc_ref[:, c0:c0 + 128] = qn.astype(BF16)
        ql_ref[:, c0:c0 + 128] = rope(qn).astype(BF16)
    kns = [_rms_rows(qkv[:, D + kh * 128:D + (kh + 1) * 128]) * gk_ref[...]
           for kh in range(GQA_KV_HEADS)]
    for kh in range(GQA_KV_HEADS):
        kk_ref[:, kh * 128:(kh + 1) * 128] = jnp.where(lat, rope(kns[kh]), kns[kh]).astype(BF16)
    v = qkv[:, D + 256:D + 512]
    vv_ref[...] = v.astype(BF16)

    @pl.when(jnp.logical_not(lat))
    def _state():
        for kh in range(GQA_KV_HEADS):
            kst_ref[:, kh * 128:(kh + 1) * 128] = kns[kh]
        vst_ref[...] = v


def _gqa_proj(x, mods, n1g, w_qkv, gq, gk, cos_t, sin_t):
    kvw = GQA_KV_HEADS * 128
    tw = functools.partial(_tok, tb=TBP)
    rope_spec = pl.BlockSpec((TBP, 128), lambda b: (_lat_blk(b, TBP), 0))
    return pl.pallas_call(
        _gqa_proj_kernel,
        out_shape=[jax.ShapeDtypeStruct((N_TOK, D), BF16),
                   jax.ShapeDtypeStruct((N_LAT_TOK, D), BF16),
                   jax.ShapeDtypeStruct((N_TOK, kvw), BF16),
                   jax.ShapeDtypeStruct((N_TOK, kvw), BF16),
                   jax.ShapeDtypeStruct((N_CTX_TOK, kvw), F32),
                   jax.ShapeDtypeStruct((N_CTX_TOK, kvw), F32)],
        grid=(N_TOK // TBP,),
        in_specs=[tw(D), _mods_spec(3, tb=TBP), _full((1, D)), _full((D, D + 2 * kvw)),
                  _full((1, 128)), _full((1, 128)), rope_spec, rope_spec],
        out_specs=[tw(D), _lat_only(D, TBP), tw(kvw), tw(kvw),
                   _ctx_only(kvw, TBP), _ctx_only(kvw, TBP)],
        compiler_params=_cparams(1),
        name="gqa_proj",
    )(x, mods, n1g, w_qkv, gq, gk, cos_t, sin_t)


REFINE_BITS = 15
F32_MIN_NORMAL = 2.0 ** -126


def _route_kernel(lg_ref, rel_ref, gate_ref, meta_ref, *, seg, n_units):
    n_tok = n_units * UNIT
    nseg = n_tok // seg
    n_chunks = n_tok // TB
    cap = seg // 8
    lg = lg_ref[...]
    ex = jnp.exp(lg - jnp.max(lg, axis=0, keepdims=True))
    aff = ex / jnp.sum(ex, axis=0, keepdims=True)
    affs = [aff[:, s * seg:(s + 1) * seg] for s in range(nseg)]
    col = lambda dtype: tuple(jnp.zeros((N_EXPERTS, 1), dtype) for _ in range(nseg))

    def count(mask):
        return jnp.sum(jnp.where(mask, 1.0, 0.0), axis=1, keepdims=True)

    def by_bits(i, bits):
        bit = jnp.left_shift(jnp.int32(1), 30 - i)
        out = []
        for s in range(nseg):
            cand = bits[s] | bit
            ok = count(affs[s] >= pltpu.bitcast(cand, F32)) >= cap
            out.append(jnp.where(ok, cand, bits[s]))
        return tuple(out)

    bits = lax.fori_loop(0, 31, by_bits, col(I32))
    encs = []
    for s in range(nseg):
        t = pltpu.bitcast(bits[s], F32)
        tn = pltpu.bitcast(bits[s] + 1, F32)
        inv = 1.0 / jnp.maximum(tn - t, F32_MIN_NORMAL)
        encs.append(jnp.where(affs[s] >= tn, 2.0, jnp.where(affs[s] >= t, (affs[s] - t) * inv, -1.0)))

    def refine(encs):
        def by_frac(i, cs):
            step = pltpu.bitcast(jnp.zeros((N_EXPERTS, 1), I32) + jnp.left_shift(126 - i, 23), F32)
            out = []
            for s in range(nseg):
                c_try = cs[s] + step
                out.append(jnp.where(count(encs[s] >= c_try) >= cap, c_try, cs[s]))
            return tuple(out)

        cs = lax.fori_loop(0, REFINE_BITS, by_frac, col(F32))
        w = 2.0 ** -REFINE_BITS
        return [jnp.where(e >= c + w, 2.0, jnp.where(e >= c, (e - c) * (1.0 / w), -1.0))
                for e, c in zip(encs, cs)]

    encs = refine(refine(encs))
    enc = jnp.concatenate(encs, axis=1) if nseg > 1 else encs[0]
    need = [cap - count(e >= 1.5) for e in encs]

    r = lax.broadcasted_iota(I32, (TB, TB), 0)
    c = lax.broadcasted_iota(I32, (TB, TB), 1)
    tri = jnp.where(r < c, 1.0, 0.0).astype(BF16)
    chunk = lambda v, cb: v[:, cb * TB:(cb + 1) * TB]
    stack = lambda masks: jnp.concatenate(
        [jnp.where(mk, 1.0, 0.0).astype(BF16) for mk in masks], axis=0)
    rows = lambda v, cb: v[cb * N_EXPERTS:(cb + 1) * N_EXPERTS]

    above = [chunk(enc, cb) >= 1.5 for cb in range(n_chunks)]
    tie = [(chunk(enc, cb) >= 0.0) & (chunk(enc, cb) < 1.5) for cb in range(n_chunks)]
    tie_rank = _bdot(stack(tie), tri)
    sel = []
    tie_carry = None
    for cb in range(n_chunks):
        if (cb * TB) % seg == 0:
            tie_carry = jnp.zeros((N_EXPERTS, 1), F32)
        keep = tie[cb] & (rows(tie_rank, cb) + tie_carry < need[(cb * TB) // seg])
        sel.append(above[cb] | keep)
        tie_carry = tie_carry + count(tie[cb])
    pos_local = _bdot(stack(sel), tri)

    lane = lax.broadcasted_iota(I32, (N_EXPERTS, 128), 1)
    lane8 = lax.broadcasted_iota(I32, (8, 128), 1)
    units_per_grp = GRP_BLK // UNIT_BLK
    for u in range(n_units):
        base = (u % units_per_grp) * (GRP_ROWS // units_per_grp)
        sel_carry = jnp.zeros((N_EXPERTS, 1), F32)
        meta_off = jnp.zeros((N_EXPERTS, 128), I32)
        meta_np = jnp.zeros((8, 128), I32)
        for j in range(UNIT_BLK):
            cb = u * UNIT_BLK + j
            pos = (rows(pos_local, cb) + sel_carry).astype(I32) + base
            off16 = (sel_carry.astype(I32) + base) & ~15
            sel_carry = sel_carry + count(sel[cb])
            rel = jnp.where(sel[cb], pos - off16, -1)
            rel_ref[:, cb * TB:(cb + 1) * TB] = rel
            gate_ref[:, cb * TB:(cb + 1) * TB] = jnp.where(sel[cb], chunk(aff, cb), 0.0)
            mx = jnp.max(jnp.max(rel, axis=1, keepdims=True), axis=0, keepdims=True)
            npass = jnp.zeros((1, 1), I32)
            for k in range(MAX_PASSES):
                npass = npass + jnp.where(mx >= k * WIN, 1, 0)
            meta_off = jnp.where(lane == j, off16, meta_off)
            meta_np = jnp.where(lane8 == j, npass, meta_np)
        meta_ref[u, 0:N_EXPERTS, :] = meta_off
        meta_ref[u, N_EXPERTS:N_EXPERTS + 8, :] = meta_np


def _route(logits, seg, name):
    n = logits.shape[1]
    n_units = n // UNIT
    return pl.pallas_call(
        functools.partial(_route_kernel, seg=seg, n_units=n_units),
        out_shape=[jax.ShapeDtypeStruct((N_EXPERTS, n), I32),
                   jax.ShapeDtypeStruct((N_EXPERTS, n), F32),
                   jax.ShapeDtypeStruct((n_units, N_EXPERTS + 8, 128), I32)],
        grid=(1,),
        in_specs=[pl.BlockSpec((N_EXPERTS, n), lambda i: (0, 0))],
        out_specs=[pl.BlockSpec((N_EXPERTS, n), lambda i: (0, 0)),
                   pl.BlockSpec((N_EXPERTS, n), lambda i: (0, 0)),
                   pl.BlockSpec((n_units, N_EXPERTS + 8, 128), lambda i: (0, 0, 0))],
        compiler_params=_cparams(1),
        name=name,
    )(logits)


def _moe_kernel(off_ref, np_ref, h2_ref, rel_ref, gt_ref, wg_ref, wu_ref, wd_ref, x1_ref,
                mods_ref, fg_ref, o_ref, xy_ref, *, final, g0, n_grp):
    g = pl.program_id(0) + g0
    s = pl.program_id(1)
    ctx_grp = N_CTX_BLK // GRP_BLK

    def by_group_kind(body):
        if g0 + n_grp <= ctx_grp:
            body(CTX_WIN, False)
        elif g0 >= ctx_grp:
            body(WIN, True)
        else:
            pl.when(g < ctx_grp)(lambda: body(CTX_WIN, False))
            pl.when(g >= ctx_grp)(lambda: body(WIN, True))

    def window_start(e, blk, p, win):
        return pl.multiple_of(off_ref[(g * N_EXPERTS + e) * GRP_BLK + blk] + p * win, 16)

    def passes(blk):
        return np_ref[g * GRP_BLK + blk]

    def one_hot(rel, p, gates, win):
        rows = lax.broadcasted_iota(I32, (win, TB), 0) + p * win
        parts = []
        for e in range(N_EXPERTS):
            hit = jnp.broadcast_to(rel[e:e + 1, :], (win, TB)) == rows
            val = 1.0 if gates is None else jnp.broadcast_to(gates[e:e + 1, :], (win, TB))
            parts.append(jnp.where(hit, val, 0.0).astype(BF16))
        return jnp.concatenate(parts, axis=0)

    @pl.when(s == 0)
    def _zero():
        xy_ref[...] = jnp.zeros(xy_ref.shape, BF16)

    def gather(win, multi_pass):
        for j in range(MOE_BPS):
            blk = s * MOE_BPS + j
            rel = rel_ref[:, j * TB:(j + 1) * TB]
            hb = h2_ref[j * TB:(j + 1) * TB, :]

            def one_pass(p):
                got = _bdot(one_hot(rel, p, None, win), hb).astype(BF16)
                for e in range(N_EXPERTS):
                    rows = pl.ds(window_start(e, blk, p, win), win)
                    xy_ref[e, rows, :] = xy_ref[e, rows, :] + got[e * win:(e + 1) * win]

            one_pass(0)
            if multi_pass:
                def extra(p, carry):
                    one_pass(p)
                    return carry

                lax.fori_loop(1, passes(blk), extra, 0)

    pl.when(s < MOE_TS)(lambda: by_group_kind(gather))

    @pl.when((s >= MOE_TS) & (s < MOE_TS + N_EXPERTS))
    def _ffn():
        e = s - MOE_TS
        x = xy_ref[e, pl.ds(0, GRP_ROWS), :]
        a = _bdot(x, wg_ref[0, 0])
        hid = (_silu(a) * _bdot(x, wu_ref[0, 0])).astype(BF16)
        xy_ref[e, pl.ds(0, GRP_ROWS), :] = _bdot(hid, wd_ref[0, 0]).astype(BF16)

    def combine(win, multi_pass):
        gate2 = mods_ref[0][:, 5 * D:6 * D]
        for j in range(MOE_BPS):
            blk = (s - (MOE_TS + N_EXPERTS)) * MOE_BPS + j
            rel = rel_ref[:, j * TB:(j + 1) * TB]
            gates = gt_ref[:, j * TB:(j + 1) * TB]

            def one_pass(p):
                wins = [xy_ref[e, pl.ds(window_start(e, blk, p, win), win), :]
                        for e in range(N_EXPERTS)]
                return lax.dot_general(one_hot(rel, p, gates, win), jnp.concatenate(wins, axis=0),
                                       TN, preferred_element_type=F32)

            y = one_pass(0)
            if multi_pass:
                y = lax.fori_loop(1, passes(blk), lambda p, acc: acc + one_pass(p), y)
            out = x1_ref[j * TB:(j + 1) * TB, :] + gate2 * y
            if final:
                out = _rms_rows(out) * fg_ref[...]
            o_ref[j * TB:(j + 1) * TB, :] = out

    pl.when(s >= MOE_TS + N_EXPERTS)(lambda: by_group_kind(combine))


def _moe_layer(layer, off16, npass, h2, rel, gates, wg, wu, wd, x1, mods, final_g, final,
               g0=0, n_grp=N_GRP):
    n_steps = 2 * MOE_TS + N_EXPERTS
    tbs = MOE_BPS * TB

    def gather_blk(g, s, *_):
        return (g0 + g) * MOE_TS + jnp.minimum(s, MOE_TS - 1)

    def route_blk(g, s, *_):
        return (g0 + g) * MOE_TS + jnp.where(s < MOE_TS + N_EXPERTS, jnp.minimum(s, MOE_TS - 1),
                                             s - (MOE_TS + N_EXPERTS))

    def combine_blk(g, s, *_):
        return (g0 + g) * MOE_TS + jnp.maximum(s - (MOE_TS + N_EXPERTS), 0)

    def expert(g, s, *_):
        return jnp.clip(s - MOE_TS, 0, N_EXPERTS - 1)

    grid_spec = pltpu.PrefetchScalarGridSpec(
        num_scalar_prefetch=2,
        grid=(n_grp, n_steps),
        in_specs=[
            pl.BlockSpec((tbs, D), lambda *a: (gather_blk(*a), 0)),
            pl.BlockSpec((N_EXPERTS, tbs), lambda *a: (0, route_blk(*a))),
            pl.BlockSpec((N_EXPERTS, tbs), lambda *a: (0, route_blk(*a))),
            pl.BlockSpec((1, 1, D, EXPERT_FF), lambda *a: (layer, expert(*a), 0, 0)),
            pl.BlockSpec((1, 1, D, EXPERT_FF), lambda *a: (layer, expert(*a), 0, 0)),
            pl.BlockSpec((1, 1, EXPERT_FF, D), lambda *a: (layer, expert(*a), 0, 0)),
            pl.BlockSpec((tbs, D), lambda *a: (combine_blk(*a), 0)),
            pl.BlockSpec((1, 1, 6 * D),
                         lambda *a: (layer * N_COND + _cond_row(combine_blk(*a), tbs), 0, 0)),
            pl.BlockSpec((1, D), lambda *a: (0, 0)),
        ],
        out_specs=pl.BlockSpec((tbs, D), lambda *a: (combine_blk(*a) - g0 * MOE_TS, 0)),
        scratch_shapes=[pltpu.VMEM((N_EXPERTS, XY_ROWS, D), BF16)],
    )
    return pl.pallas_call(
        functools.partial(_moe_kernel, final=final, g0=g0, n_grp=n_grp),
        out_shape=jax.ShapeDtypeStruct((n_grp * GRP_BLK * TB, D), F32),
        grid_spec=grid_spec,
        compiler_params=_cparams(2),
        name="moe_experts",
    )(off16, npass, h2, rel, gates, wg, wu, wd, x1, mods, final_g)


def _moe(layer, x1, h2, logits, wg, wu, wd, mods, final_g, final):
    rel_c, gate_c, meta_c = _route(logits[:, :N_CTX_TOK], CTX_LEN, "route_ctx")
    rel_l, gate_l, meta_l = _route(logits[:, N_CTX_TOK:], LAT_LEN, "route_lat")
    rel = jnp.concatenate([rel_c, rel_l], axis=1)
    gates = jnp.concatenate([gate_c, gate_l], axis=1)
    meta = jnp.concatenate([meta_c, meta_l], axis=0)
    upg = GRP_BLK // UNIT_BLK
    off16 = meta[:, :N_EXPERTS, :UNIT_BLK].reshape(N_GRP, upg, N_EXPERTS, UNIT_BLK)
    off16 = jnp.swapaxes(off16, 1, 2).reshape(-1)
    npass = meta[:, N_EXPERTS, :UNIT_BLK].reshape(-1)
    args = (layer, off16, npass, h2, rel, gates, wg, wu, wd, x1, mods, final_g, final)
    if not final:
        return _moe_layer(*args)
    ctx_grp = N_CTX_BLK // GRP_BLK
    return _moe_layer(*args, g0=0, n_grp=ctx_grp), _moe_layer(*args, g0=ctx_grp, n_grp=N_GRP - ctx_grp)


def _axial_tables(rot_dim):
    t = jnp.arange(LAT_LEN)
    row = (t // GRID_W).astype(F32)
    col = (t % GRID_W).astype(F32)
    axis_dim = rot_dim // 2
    inv = ROPE_THETA ** (-jnp.arange(0, axis_dim, 2, dtype=F32) / axis_dim)
    ang = jnp.concatenate([row[:, None] * inv, col[:, None] * inv], axis=-1)
    return jnp.cos(ang), jnp.sin(ang)


def kernel(x_prompt, x_sample, c, cache_mla_ckv, cache_mla_krope, cache_gqa_k, cache_gqa_v, c_ctx, ada_w, ada_b, norm1_g, norm2_g, final_g, pool_w, pool_scale, mla_w_down, mla_g_q, mla_w_uq, mla_g_kv, mla_w_ukv, mla_w_o, conv_w_pw1, conv_b_pw1, conv_w_dw, conv_b_dw, conv_ln_g, conv_ln_b, conv_w_pw2, conv_b_pw2, gqa_w_qkv, gqa_g_q, gqa_g_k, gqa_w_o, router_w, moe_w_gate, moe_w_up, moe_w_down):
    cond =jnp.zeros((N_COND, D), F32).at[0].set(c_ctx).at[1:1 + N_LAT_SEQ].set(c)
    mods = _adaln(cond, ada_w, ada_b).reshape(DEPTH * N_COND, 1, 6 * D)

    row = lambda v: v.reshape(1, -1)
    rw_t = jnp.swapaxes(router_w, 1, 2)
    rwh = rw_t.astype(BF16)
    rw2 = jnp.concatenate([rwh, (rw_t - rwh.astype(F32)).astype(BF16)], axis=1)
    wg, wu, wd = moe_w_gate.astype(BF16), moe_w_up.astype(BF16), moe_w_down.astype(BF16)
    fg = row(final_g)

    def post_args(l):
        return row(norm2_g[l]), rw2[l]

    def moe(l, x1, h2, lg):
        return _moe(l, x1, h2, lg, wg, wu, wd, mods, fg, l == DEPTH - 1)

    x1, h2, lg = _pool_layer(x_prompt.reshape(N_CTX_TOK, D), x_sample.reshape(N_LAT_TOK, D), mods,
                             row(norm1_g[0]), pool_w[0].astype(BF16), row(pool_scale[0]),
                             *post_args(0))
    x = moe(0, x1, h2, lg)

    cos, sin = _axial_tables(MLA_ROPE)
    z = jnp.zeros((LAT_LEN, 64), F32)
    cos_t = jnp.concatenate([cos, cos, z], axis=1)
    sin_t = jnp.concatenate([-sin, sin, z], axis=1)
    w_down = jnp.pad(mla_w_down[0], ((0, 0), (0, MLA_DOWN_PAD - mla_w_down.shape[2]))).astype(BF16)
    w_uq = jnp.pad(mla_w_uq[0].reshape(MLA_Q_LORA, MLA_HEADS, MLA_NOPE + MLA_ROPE),
                   ((0, 0), (0, 0), (0, MLA_QK - MLA_NOPE - MLA_ROPE)))
    w_uq = w_uq.reshape(MLA_Q_LORA, MLA_HEADS * MLA_QK).astype(BF16)
    w_ukv = mla_w_ukv[0].astype(BF16)
    qc, ql, kmat, vmat, ckv_n, krp = _mla_proj(
        x, mods, row(norm1_g[1]), w_down, row(mla_g_q[0]), w_uq, row(mla_g_kv[0]), w_ukv, cos_t, sin_t)
    cache_kr = jnp.pad(cache_mla_krope[:, 0].reshape(N_LAT_SEQ * PAST_LEN, MLA_ROPE),
                       ((0, 0), (0, 128 - MLA_ROPE)))
    kc, vc = _mla_cache(cache_mla_ckv[:, 0].reshape(N_LAT_SEQ * PAST_LEN, MLA_KV_LORA), cache_kr, w_ukv)
    a_ctx = _attn_ctx(qc, kmat, vmat, MLA_HEADS, MLA_HEADS, MLA_QK, "mla_attn_ctx")
    a_lat = _attn_lat(qc, ql, kc, kmat, vc, vmat, MLA_HEADS, MLA_HEADS, MLA_QK, "mla_attn_lat")
    x1, h2, lg = _oproj_layer(1, a_ctx, a_lat, mla_w_o[0].astype(BF16), x, mods, *post_args(1))
    x = moe(1, x1, h2, lg)
    state_ckv = ckv_n.reshape(N_CTX_SEQ, 1, CTX_LEN, MLA_KV_LORA)
    state_kr = krp.reshape(N_CTX_SEQ, 1, CTX_LEN, MLA_ROPE)

    x1, h2, lg = _conv_layer(x, mods, row(norm1_g[2]), conv_w_pw1[0].astype(BF16), row(conv_b_pw1[0]),
                             conv_w_dw[0], row(conv_b_dw[0]), row(conv_ln_g[0]), row(conv_ln_b[0]),
                             conv_w_pw2[0].astype(BF16), row(conv_b_pw2[0]), *post_args(2))
    x = moe(2, x1, h2, lg)

    cos, sin = _axial_tables(GQA_HEAD_DIM)
    cos_t = jnp.concatenate([cos, cos], axis=1)
    sin_t = jnp.concatenate([-sin, sin], axis=1)
    qc, ql, kk, vv, kst, vst = _gqa_proj(x, mods, row(norm1_g[3]), gqa_w_qkv[0].astype(BF16),
                                         row(gqa_g_q[0]), row(gqa_g_k[0]), cos_t, sin_t)
    kvw = GQA_KV_HEADS * GQA_HEAD_DIM
    kc = cache_gqa_k[:, 0].reshape(N_LAT_SEQ * PAST_LEN, kvw).astype(BF16)
    vc = cache_gqa_v[:, 0].reshape(N_LAT_SEQ * PAST_LEN, kvw).astype(BF16)
    a_ctx = _attn_ctx(qc, kk, vv, GQA_HEADS, GQA_KV_HEADS, GQA_HEAD_DIM, "gqa_attn_ctx")
    a_lat = _attn_lat(qc, ql, kc, kk, vc, vv, GQA_HEADS, GQA_KV_HEADS, GQA_HEAD_DIM, "gqa_attn_lat")
    x1, h2, lg = _oproj_layer(3, a_ctx, a_lat, gqa_w_o[0].astype(BF16), x, mods, *post_args(3))
    y_ctx, y_lat = moe(3, x1, h2, lg)
    state_k = kst.reshape(N_CTX_SEQ, 1, CTX_LEN, GQA_KV_HEADS, GQA_HEAD_DIM)
    state_v = vst.reshape(N_CTX_SEQ, 1, CTX_LEN, GQA_KV_HEADS, GQA_HEAD_DIM)

    y_prompt = y_ctx.reshape(N_CTX_SEQ, CTX_LEN, D)
    y_sample = y_lat.reshape(N_LAT_SEQ, LAT_LEN, D)
    return (y_prompt, y_sample, state_ckv, state_kr, state_k, state_v)
```

```python
import functools
import math

import jax
import jax.numpy as jnp
from jax import lax
from jax.experimental import pallas as pl
from jax.experimental.pallas import tpu as pltpu

F32 = jnp.float32
BF16 = jnp.bfloat16
I32 = jnp.int32

D = 1024
DEPTH = 4
EPS = 1e-6
GRID_W = 64
ROPE_THETA = 10000.0
LOG2E = math.log2(math.e)

TB = 256
TBW = 512
TBP = 256
HALO = 16
N_CTX_SEQ, CTX_LEN = 32, 256
N_LAT_SEQ, LAT_LEN = 8, 2048
PAST_LEN = 512
N_CTX_TOK = N_CTX_SEQ * CTX_LEN
N_LAT_TOK = N_LAT_SEQ * LAT_LEN
N_TOK = N_CTX_TOK + N_LAT_TOK
N_CTX_BLK = N_CTX_TOK // TB
LAT_BLK = LAT_LEN // TB
N_BLK = N_TOK // TB
N_COND = 16

POOL_WINDOWS = (2, 4, 8, 16)
POOL_GW = 256
MLA_HEADS = 8
MLA_Q_LORA = 384
MLA_KV_LORA = 256
MLA_NOPE = 128
MLA_ROPE = 64
MLA_DOWN_PAD = 768
MLA_QK = 256
MLA_SCALE = (MLA_NOPE + MLA_ROPE) ** -0.5
CONV_WIDTH = 31
GQA_HEADS = 8
GQA_KV_HEADS = 2
GQA_HEAD_DIM = 128
GQA_SCALE = GQA_HEAD_DIM ** -0.5
N_EXPERTS = 16
EXPERT_FF = 512

UNIT = 2048
UNIT_BLK = UNIT // TB
GRP_BLK = 16
N_GRP = N_BLK // GRP_BLK
GRP_ROWS = 512
WIN = 64
MAX_PASSES = -(-(TB + 15) // WIN)
CTX_WIN = CTX_LEN // 8
XY_ROWS = GRP_ROWS + WIN
MOE_BPS = 2
MOE_TS = GRP_BLK // MOE_BPS

NT = (((1,), (1,)), ((), ()))
TN = (((0,), (0,)), ((), ()))

VMEM_LIMIT = 56 * 1024 * 1024


def _cparams(n_axes):
    return pltpu.CompilerParams(dimension_semantics=("arbitrary",) * n_axes,
                                vmem_limit_bytes=VMEM_LIMIT)


def _cond_row(b, tb=TB):
    n_ctx = N_CTX_TOK // tb
    return jnp.where(b < n_ctx, 0, 1 + (b - n_ctx) // (LAT_LEN // tb))


def _lat_blk(b, tb=TB):
    n_ctx = N_CTX_TOK // tb
    return jnp.where(b < n_ctx, 0, (b - n_ctx) % (LAT_LEN // tb))


def _mods_spec(layer, blk_of=lambda *g: g[0], tb=TB):
    return pl.BlockSpec((1, 1, 6 * D),
                        lambda *g: (layer * N_COND + _cond_row(blk_of(*g), tb), 0, 0))


def _full(shape):
    return pl.BlockSpec(shape, lambda *g: (0,) * len(shape))


def _tok(width, blk_of=lambda *g: g[0], tb=TB):
    return pl.BlockSpec((tb, width), lambda *g: (blk_of(*g), 0))


def _ctx_only(width, tb=TB):
    return pl.BlockSpec((tb, width), lambda b: (jnp.minimum(b, N_CTX_TOK // tb - 1), 0))


def _lat_only(width, tb=TB):
    return pl.BlockSpec((tb, width), lambda b: (jnp.maximum(b - N_CTX_TOK // tb, 0), 0))


def _halo_specs(width):
    per = TB // HALO
    prev = pl.BlockSpec((HALO, width), lambda b: (jnp.maximum(b * per - 1, 0), 0))
    nxt = pl.BlockSpec((HALO, width), lambda b: (jnp.minimum((b + 1) * per, N_TOK // HALO - 1), 0))
    return prev, nxt


def _seq_edges(b):
    lat = b >= N_CTX_BLK
    j = _lat_blk(b)
    return lat & (j > 0), lat & (j < LAT_BLK - 1)


def _rms_rows(x):
    return x * lax.rsqrt(jnp.mean(x * x, axis=-1, keepdims=True) + EPS)


def _modulate(x, g, shift, scale):
    return _rms_rows(x) * g * (1.0 + scale) + shift


def _silu(x):
    return x * jax.nn.sigmoid(x)


def _bdot(a, b):
    return jnp.dot(a, b, preferred_element_type=F32)


def _adaln_kernel(c_ref, w_ref, b_ref, o_ref):
    s = _silu(c_ref[...]).astype(BF16)
    o_ref[0] = _bdot(s, w_ref[0].astype(BF16)) + b_ref[0]


def _adaln(cond, ada_w, ada_b):
    tn = 1536
    return pl.pallas_call(
        _adaln_kernel,
        out_shape=jax.ShapeDtypeStruct((DEPTH, N_COND, 6 * D), F32),
        grid=(DEPTH, 6 * D // tn),
        in_specs=[
            _full((N_COND, D)),
            pl.BlockSpec((1, D, tn), lambda l, n: (l, 0, n)),
            pl.BlockSpec((1, 1, tn), lambda l, n: (l, 0, n)),
        ],
        out_specs=pl.BlockSpec((1, N_COND, tn), lambda l, n: (l, 0, n)),
        compiler_params=_cparams(2),
        name="adaln",
    )(cond, ada_w, ada_b.reshape(DEPTH, 1, 6 * D))


def _post(x, y, m, n2g_ref, rw_ref, x1_ref, h2_ref, lg_ref, rows=slice(None)):
    x1 = x + m[:, 2 * D:3 * D] * y
    x1_ref[rows, :] = x1
    h2 = _modulate(x1, n2g_ref[...], m[:, 3 * D:4 * D], m[:, 4 * D:5 * D])
    hh = h2.astype(BF16)
    hl = (h2 - hh.astype(F32)).astype(BF16)
    h2_ref[rows, :] = hh
    n = x.shape[0]
    r = lax.dot_general(rw_ref[...], jnp.concatenate([hh, hl], axis=0), NT,
                        preferred_element_type=F32)
    lg_ref[:, rows] = r[:N_EXPERTS, :n] + r[N_EXPERTS:, :n] + r[:N_EXPERTS, n:]


_POST_IN = [_full((1, D)), _full((2 * N_EXPERTS, D))]
_POST_OUT_SHAPES = [jax.ShapeDtypeStruct((N_TOK, D), F32),
                    jax.ShapeDtypeStruct((N_TOK, D), BF16),
                    jax.ShapeDtypeStruct((N_EXPERTS, N_TOK), F32)]


def _post_out_specs(tb=TB):
    return [_tok(D, tb=tb), _tok(D, tb=tb), pl.BlockSpec((N_EXPERTS, tb), lambda b: (0, b))]


def _pool_kernel(xc_ref, xl_ref, xp_ref, xn_ref, mods_ref, n1g_ref, pw_ref, ps_ref,
                 n2g_ref, rw_ref, x1_ref, h2_ref, lg_ref):
    b = pl.program_id(0)
    lat = b >= N_CTX_TOK // TBW
    j = _lat_blk(b, TBW)
    m = mods_ref[0]
    g, sh, sc = n1g_ref[...], m[:, 0:D], m[:, D:2 * D]
    x = jnp.where(lat, xl_ref[...], xc_ref[...])
    h = _modulate(x, g, sh, sc)
    hext = jnp.concatenate([_modulate(xp_ref[...], g, sh, sc), h,
                            _modulate(xn_ref[...], g, sh, sc)], axis=0)
    ext = TB + 2 * HALO
    n_sub = TBW // TB
    t = lax.broadcasted_iota(I32, (TB, ext), 0)
    u = lax.broadcasted_iota(I32, (TB, ext), 1) - HALO
    tc = lax.broadcasted_iota(I32, (TB, 1), 0)
    for i in range(n_sub):
        has_prev = lat & (j > 0) if i == 0 else lat
        has_next = lat & (j < LAT_LEN // TBW - 1) if i == n_sub - 1 else lat
        valid = ((u >= 0) | has_prev) & ((u < TB) | has_next)
        lo_lim = jnp.where(has_prev, -HALO, 0)
        hi_lim = jnp.where(has_next, TB + HALO, TB)
        he = hext[i * TB:i * TB + ext]
        hc = h[i * TB:(i + 1) * TB]
        ys = []
        for gi, win in enumerate(POOL_WINDOWS):
            hw = win // 2
            band = jnp.where((u >= t - hw) & (u < t + hw) & valid, 1.0, 0.0).astype(BF16)
            cnt = (jnp.minimum(tc + hw, hi_lim) - jnp.maximum(tc - hw, lo_lim)).astype(F32)
            hg = he[:, gi * POOL_GW:(gi + 1) * POOL_GW]
            hi = hg.astype(BF16)
            lo = (hg - hi.astype(F32)).astype(BF16)
            tot = _bdot(band, hi) + _bdot(band, lo)
            pooled = tot / cnt - hc[:, gi * POOL_GW:(gi + 1) * POOL_GW]
            ys.append(_bdot(pooled.astype(BF16), pw_ref[gi]))
        y = jnp.concatenate(ys, axis=1) * ps_ref[...]
        _post(x[i * TB:(i + 1) * TB], y, m, n2g_ref, rw_ref, x1_ref, h2_ref, lg_ref,
              pl.ds(i * TB, TB))


def _pool_layer(x_ctx, x_lat, mods, n1g, pool_w, pool_scale, n2g, rw):
    per = TBW // HALO
    lat_b = lambda b: jnp.maximum(b - N_CTX_TOK // TBW, 0)
    prev = pl.BlockSpec((HALO, D), lambda b: (jnp.maximum(lat_b(b) * per - 1, 0), 0))
    nxt = pl.BlockSpec((HALO, D),
                       lambda b: (jnp.minimum((lat_b(b) + 1) * per, N_LAT_TOK // HALO - 1), 0))
    return pl.pallas_call(
        _pool_kernel,
        out_shape=_POST_OUT_SHAPES,
        grid=(N_TOK // TBW,),
        in_specs=[_ctx_only(D, TBW), _lat_only(D, TBW), prev, nxt, _mods_spec(0, tb=TBW),
                  _full((1, D)), _full((4, POOL_GW, POOL_GW)), _full((1, D))] + _POST_IN,
        out_specs=_post_out_specs(TBW),
        compiler_params=_cparams(1),
        name="pool_mixer",
    )(x_ctx, x_lat, x_lat, x_lat, mods, n1g, pool_w, pool_scale, n2g, rw)


def _rope_half32(v, cos_t, sin_t):
    lane = lax.broadcasted_iota(I32, v.shape, 1)
    swapped = jnp.where(lane < 32, pltpu.roll(v, 96, axis=1), pltpu.roll(v, 32, axis=1))
    return v * cos_t + swapped * sin_t


def _mla_proj_kernel(x_ref, mods_ref, n1g_ref, wd_ref, gq_ref, wuq_ref, gkv_ref, wukv_ref,
                     cos_ref, sin_ref, qc_ref, ql_ref, k_ref, v_ref, ckv_ref, kr_ref):
    lat = pl.program_id(0) >= N_CTX_TOK // TBP
    m = mods_ref[0]
    h = _modulate(x_ref[...], n1g_ref[...], m[:, 0:D], m[:, D:2 * D])
    d = _bdot(h.astype(BF16), wd_ref[...])
    cq = d[:, :MLA_Q_LORA]
    ckv = d[:, MLA_Q_LORA:MLA_Q_LORA + MLA_KV_LORA]
    krp = d[:, MLA_Q_LORA + MLA_KV_LORA:]
    q = _bdot((_rms_rows(cq) * gq_ref[...]).astype(BF16), wuq_ref[...])
    ckv_n = _rms_rows(ckv) * gkv_ref[...]
    kv = _bdot(ckv_n.astype(BF16), wukv_ref[...])
    cos_t, sin_t = cos_ref[...], sin_ref[...]
    k_rope = jnp.where(lat, _rope_half32(krp, cos_t, sin_t), krp).astype(BF16)
    qs = MLA_SCALE * LOG2E
    for hd in range(MLA_HEADS):
        c0 = hd * MLA_QK
        qn = (q[:, c0:c0 + 128] * qs).astype(BF16)
        qr = q[:, c0 + 128:c0 + 256] * qs
        qc_ref[:, c0:c0 + 128] = qn
        qc_ref[:, c0 + 128:c0 + 256] = qr.astype(BF16)
        ql_ref[:, c0:c0 + 128] = qn
        ql_ref[:, c0 + 128:c0 + 256] = _rope_half32(qr, cos_t, sin_t).astype(BF16)
        k_ref[:, c0:c0 + 128] = kv[:, c0:c0 + 128].astype(BF16)
        k_ref[:, c0 + 128:c0 + 256] = k_rope
        v_ref[:, hd * 128:(hd + 1) * 128] = kv[:, c0 + 128:c0 + 256].astype(BF16)

    @pl.when(jnp.logical_not(lat))
    def _state():
        ckv_ref[...] = ckv_n
        kr_ref[...] = krp[:, :MLA_ROPE]


def _mla_proj(x, mods, n1g, wd, gq, wuq, gkv, wukv, cos_t, sin_t):
    hq = MLA_HEADS * MLA_QK
    tw = functools.partial(_tok, tb=TBP)
    rope_spec = pl.BlockSpec((TBP, 128), lambda b: (_lat_blk(b, TBP), 0))
    return pl.pallas_call(
        _mla_proj_kernel,
        out_shape=[jax.ShapeDtypeStruct((N_TOK, hq), BF16),
                   jax.ShapeDtypeStruct((N_LAT_TOK, hq), BF16),
                   jax.ShapeDtypeStruct((N_TOK, hq), BF16),
                   jax.ShapeDtypeStruct((N_TOK, MLA_HEADS * 128), BF16),
                   jax.ShapeDtypeStruct((N_CTX_TOK, MLA_KV_LORA), F32),
                   jax.ShapeDtypeStruct((N_CTX_TOK, MLA_ROPE), F32)],
        grid=(N_TOK // TBP,),
        in_specs=[tw(D), _mods_spec(1, tb=TBP), _full((1, D)), _full((D, MLA_DOWN_PAD)),
                  _full((1, MLA_Q_LORA)), _full((MLA_Q_LORA, hq)), _full((1, MLA_KV_LORA)),
                  _full((MLA_KV_LORA, hq)), rope_spec, rope_spec],
        out_specs=[tw(hq), _lat_only(hq, TBP), tw(hq), tw(MLA_HEADS * 128),
                   _ctx_only(MLA_KV_LORA, TBP), _ctx_only(MLA_ROPE, TBP)],
        compiler_params=_cparams(1),
        name="mla_proj",
    )(x, mods, n1g, wd, gq, wuq, gkv, wukv, cos_t, sin_t)


def _mla_cache_kernel(ckv_ref, kr_ref, wukv_ref, k_ref, v_ref):
    kv = _bdot(ckv_ref[...].astype(BF16), wukv_ref[...])
    kr = kr_ref[...].astype(BF16)
    for hd in range(MLA_HEADS):
        c0 = hd * MLA_QK
        k_ref[:, c0:c0 + 128] = kv[:, c0:c0 + 128].astype(BF16)
        k_ref[:, c0 + 128:c0 + 256] = kr
        v_ref[:, hd * 128:(hd + 1) * 128] = kv[:, c0 + 128:c0 + 256].astype(BF16)


def _mla_cache(ckv, krp, wukv):
    n = ckv.shape[0]
    hq = MLA_HEADS * MLA_QK
    return pl.pallas_call(
        _mla_cache_kernel,
        out_shape=[jax.ShapeDtypeStruct((n, hq), BF16),
                   jax.ShapeDtypeStruct((n, MLA_HEADS * 128), BF16)],
        grid=(n // TB,),
        in_specs=[_tok(MLA_KV_LORA), _tok(128), _full((MLA_KV_LORA, hq))],
        out_specs=[_tok(hq), _tok(MLA_HEADS * 128)],
        compiler_params=_cparams(1),
        name="mla_cache_kv",
    )(ckv, krp, wukv)


def _attn_ctx_kernel(q_ref, k_ref, v_ref, o_ref, *, heads, kv_heads, dk):
    rep = heads // kv_heads
    for hd in range(heads):
        kh = hd // rep
        q = q_ref[:, hd * dk:(hd + 1) * dk]
        k = k_ref[:, kh * dk:(kh + 1) * dk]
        s = lax.dot_general(q, k, NT, preferred_element_type=F32)
        p = jnp.exp2(s - jnp.max(s, axis=-1, keepdims=True))
        l = jnp.sum(p, axis=-1, keepdims=True)
        o = _bdot(p.astype(BF16), v_ref[:, kh * 128:(kh + 1) * 128])
        o_ref[:, hd * 128:(hd + 1) * 128] = (o / l).astype(BF16)


def _attn_ctx(q, k, v, heads, kv_heads, dk, name):
    return pl.pallas_call(
        functools.partial(_attn_ctx_kernel, heads=heads, kv_heads=kv_heads, dk=dk),
        out_shape=jax.ShapeDtypeStruct((N_CTX_TOK, heads * 128), BF16),
        grid=(N_CTX_SEQ,),
        in_specs=[_tok(heads * dk), _tok(kv_heads * dk), _tok(kv_heads * 128)],
        out_specs=_tok(heads * 128),
        compiler_params=_cparams(1),
        name=name,
    )(q, k, v)


ATT_TQ = 2048


ATT_SUB = 256


def _attn_lat_kernel(qc_ref, ql_ref, kc_ref, kl_ref, vc_ref, vl_ref, o_ref):
    for i in range(ATT_TQ // ATT_SUB):
        rows = pl.ds(i * ATT_SUB, ATT_SUB)
        sc = lax.dot_general(qc_ref[rows, :], kc_ref[...], NT, preferred_element_type=F32)
        sl = lax.dot_general(ql_ref[rows, :], kl_ref[...], NT, preferred_element_type=F32)
        mx = jnp.maximum(jnp.max(sc, axis=-1, keepdims=True), jnp.max(sl, axis=-1, keepdims=True))
        pc = jnp.exp2(sc - mx)
        pn = jnp.exp2(sl - mx)
        l = jnp.sum(pc, axis=-1, keepdims=True) + jnp.sum(pn, axis=-1, keepdims=True)
        o = _bdot(pc.astype(BF16), vc_ref[...]) + _bdot(pn.astype(BF16), vl_ref[...])
        o_ref[rows, :] = (o / l).astype(BF16)


def _attn_lat(qc, ql, kc, kl, vc, vl, heads, kv_heads, dk, name):
    rep = heads // kv_heads
    nq = LAT_LEN // ATT_TQ
    ctx_q_blk = N_CTX_TOK // ATT_TQ
    ctx_kv_blk = N_CTX_TOK // LAT_LEN
    return pl.pallas_call(
        _attn_lat_kernel,
        out_shape=jax.ShapeDtypeStruct((N_LAT_TOK, heads * 128), BF16),
        grid=(N_LAT_SEQ, heads, nq),
        in_specs=[
            pl.BlockSpec((ATT_TQ, dk), lambda b, h, i: (ctx_q_blk + b * nq + i, h)),
            pl.BlockSpec((ATT_TQ, dk), lambda b, h, i: (b * nq + i, h)),
            pl.BlockSpec((PAST_LEN, dk), lambda b, h, i: (b, h // rep)),
            pl.BlockSpec((LAT_LEN, dk), lambda b, h, i: (ctx_kv_blk + b, h // rep)),
            pl.BlockSpec((PAST_LEN, 128), lambda b, h, i: (b, h // rep)),
            pl.BlockSpec((LAT_LEN, 128), lambda b, h, i: (ctx_kv_blk + b, h // rep)),
        ],
        out_specs=pl.BlockSpec((ATT_TQ, 128), lambda b, h, i: (b * nq + i, h)),
        compiler_params=_cparams(3),
        name=name,
    )(qc, ql, kc, kl, vc, vl)


def _oproj_kernel(ac_ref, al_ref, w_ref, x_ref, mods_ref,
                  n2g_ref, rw_ref, x1_ref, h2_ref, lg_ref):
    lat = pl.program_id(0) >= N_CTX_TOK // TBW
    a = jnp.where(lat, al_ref[...], ac_ref[...])
    y = _bdot(a, w_ref[...])
    _post(x_ref[...], y, mods_ref[0], n2g_ref, rw_ref, x1_ref, h2_ref, lg_ref)


def _oproj_layer(layer, a_ctx, a_lat, w_o, x, mods, n2g, rw):
    return pl.pallas_call(
        _oproj_kernel,
        out_shape=_POST_OUT_SHAPES,
        grid=(N_TOK // TBW,),
        in_specs=[_ctx_only(D, TBW), _lat_only(D, TBW), _full((D, D)), _tok(D, tb=TBW),
                  _mods_spec(layer, tb=TBW)] + _POST_IN,
        out_specs=_post_out_specs(TBW),
        compiler_params=_cparams(1),
        name="attn_out_proj",
    )(a_ctx, a_lat, w_o, x, mods, n2g, rw)


def _conv_kernel(x_ref, xp_ref, xn_ref, mods_ref, n1g_ref, w1_ref, b1_ref, wdw_ref, bdw_ref,
                 lng_ref, lnb_ref, w2_ref, b2_ref, n2g_ref, rw_ref, x1_ref, h2_ref, lg_ref):
    b = pl.program_id(0)
    has_prev, has_next = _seq_edges(b)
    m = mods_ref[0]
    x = x_ref[...]
    xcat = jnp.concatenate([xp_ref[...], x, xn_ref[...]], axis=0)
    h = _modulate(xcat, n1g_ref[...], m[:, 0:D], m[:, D:2 * D]).astype(BF16)
    ext = TB + 2 * HALO
    row = lax.broadcasted_iota(I32, (ext, 1), 0)
    inside = ((row >= HALO) | has_prev) & ((row < HALO + TB) | has_next)
    wdw = wdw_ref[...]
    half = CONV_WIDTH // 2
    cols = []
    for p0 in range(0, D, 256):
        val = _bdot(h, w1_ref[:, p0:p0 + 256]) + b1_ref[:, p0:p0 + 256]
        gate = _bdot(h, w1_ref[:, D + p0:D + p0 + 256]) + b1_ref[:, D + p0:D + p0 + 256]
        ucat = jnp.where(inside, val * jax.nn.sigmoid(gate), 0.0)
        shifted = [ucat if r == 0 else pltpu.roll(ucat, ext - r, axis=0) for r in range(8)]
        for c0 in range(0, 256, 128):
            acc = jnp.zeros((TB, 128), F32)
            for k in range(CONV_WIDTH):
                a, r = divmod(k + HALO - half, 8)
                acc = acc + (shifted[r][8 * a:8 * a + TB, c0:c0 + 128]
                             * wdw[k:k + 1, p0 + c0:p0 + c0 + 128])
            cols.append(acc)
    u = jnp.concatenate(cols, axis=1) + bdw_ref[...]
    mu = jnp.mean(u, axis=-1, keepdims=True)
    uc = u - mu
    var = jnp.mean(uc * uc, axis=-1, keepdims=True)
    un_ = uc * lax.rsqrt(var + EPS) * lng_ref[...] + lnb_ref[...]
    y = _bdot(_silu(un_).astype(BF16), w2_ref[...]) + b2_ref[...]
    _post(x, y, m, n2g_ref, rw_ref, x1_ref, h2_ref, lg_ref)


def _conv_layer(x, mods, n1g, w_pw1, b_pw1, w_dw, b_dw, ln_g, ln_b, w_pw2, b_pw2, n2g, rw):
    prev, nxt = _halo_specs(D)
    return pl.pallas_call(
        _conv_kernel,
        out_shape=_POST_OUT_SHAPES,
        grid=(N_BLK,),
        in_specs=[_tok(D), prev, nxt, _mods_spec(2), _full((1, D)), _full((D, 2 * D)),
                  _full((1, 2 * D)), _full((CONV_WIDTH, D)), _full((1, D)), _full((1, D)),
                  _full((1, D)), _full((D, D)), _full((1, D))] + _POST_IN,
        out_specs=_post_out_specs(),
        compiler_params=_cparams(1),
        name="conv_module",
    )(x, x, x, mods, n1g, w_pw1, b_pw1, w_dw, b_dw, ln_g, ln_b, w_pw2, b_pw2, n2g, rw)


def _gqa_proj_kernel(x_ref, mods_ref, n1g_ref, w_ref, gq_ref, gk_ref, cos_ref, sin_ref,
                     qc_ref, ql_ref, kk_ref, vv_ref, kst_ref, vst_ref):
    lat = pl.program_id(0) >= N_CTX_TOK // TBP
    m = mods_ref[0]
    h = _modulate(x_ref[...], n1g_ref[...], m[:, 0:D], m[:, D:2 * D])
    qkv = _bdot(h.astype(BF16), w_ref[...])
    cos_t, sin_t = cos_ref[...], sin_ref[...]
    rope = lambda v: v * cos_t + pltpu.roll(v, 64, axis=1) * sin_t
    qs = GQA_SCALE * LOG2E
    for hd in range(GQA_HEADS):
        c0 = hd * 128
        qn = _rms_rows(qkv[:, c0:c0 + 128]) * gq_ref[...] * qs
        qc_ref[:, c0:c0 + 128] = qn.astype(BF16)
        ql_ref[:, c0:c0 + 128] = rope(qn).astype(BF16)
    kns = [_rms_rows(qkv[:, D + kh * 128:D + (kh + 1) * 128]) * gk_ref[...]
           for kh in range(GQA_KV_HEADS)]
    for kh in range(GQA_KV_HEADS):
        kk_ref[:, kh * 128:(kh + 1) * 128] = jnp.where(lat, rope(kns[kh]), kns[kh]).astype(BF16)
    v = qkv[:, D + 256:D + 512]
    vv_ref[...] = v.astype(BF16)

    @pl.when(jnp.logical_not(lat))
    def _state():
        for kh in range(GQA_KV_HEADS):
            kst_ref[:, kh * 128:(kh + 1) * 128] = kns[kh]
        vst_ref[...] = v


def _gqa_proj(x, mods, n1g, w_qkv, gq, gk, cos_t, sin_t):
    kvw = GQA_KV_HEADS * 128
    tw = functools.partial(_tok, tb=TBP)
    rope_spec = pl.BlockSpec((TBP, 128), lambda b: (_lat_blk(b, TBP), 0))
    return pl.pallas_call(
        _gqa_proj_kernel,
        out_shape=[jax.ShapeDtypeStruct((N_TOK, D), BF16),
                   jax.ShapeDtypeStruct((N_LAT_TOK, D), BF16),
                   jax.ShapeDtypeStruct((N_TOK, kvw), BF16),
                   jax.ShapeDtypeStruct((N_TOK, kvw), BF16),
                   jax.ShapeDtypeStruct((N_CTX_TOK, kvw), F32),
                   jax.ShapeDtypeStruct((N_CTX_TOK, kvw), F32)],
        grid=(N_TOK // TBP,),
        in_specs=[tw(D), _mods_spec(3, tb=TBP), _full((1, D)), _full((D, D + 2 * kvw)),
                  _full((1, 128)), _full((1, 128)), rope_spec, rope_spec],
        out_specs=[tw(D), _lat_only(D, TBP), tw(kvw), tw(kvw),
                   _ctx_only(kvw, TBP), _ctx_only(kvw, TBP)],
        compiler_params=_cparams(1),
        name="gqa_proj",
    )(x, mods, n1g, w_qkv, gq, gk, cos_t, sin_t)


REFINE_BITS = 15
F32_MIN_NORMAL = 2.0 ** -126


def _route_kernel(lg_ref, rel_ref, gate_ref, meta_ref, *, seg, n_units):
    n_tok = n_units * UNIT
    nseg = n_tok // seg
    n_chunks = n_tok // TB
    cap = seg // 8
    lg = lg_ref[...]
    ex = jnp.exp(lg - jnp.max(lg, axis=0, keepdims=True))
    aff = ex / jnp.sum(ex, axis=0, keepdims=True)
    affs = [aff[:, s * seg:(s + 1) * seg] for s in range(nseg)]
    col = lambda dtype: tuple(jnp.zeros((N_EXPERTS, 1), dtype) for _ in range(nseg))

    def count(mask):
        return jnp.sum(jnp.where(mask, 1.0, 0.0), axis=1, keepdims=True)

    def by_bits(i, bits):
        bit = jnp.left_shift(jnp.int32(1), 30 - i)
        out = []
        for s in range(nseg):
            cand = bits[s] | bit
            ok = count(affs[s] >= pltpu.bitcast(cand, F32)) >= cap
            out.append(jnp.where(ok, cand, bits[s]))
        return tuple(out)

    bits = lax.fori_loop(0, 31, by_bits, col(I32))
    encs = []
    for s in range(nseg):
        t = pltpu.bitcast(bits[s], F32)
        tn = pltpu.bitcast(bits[s] + 1, F32)
        inv = 1.0 / jnp.maximum(tn - t, F32_MIN_NORMAL)
        encs.append(jnp.where(affs[s] >= tn, 2.0, jnp.where(affs[s] >= t, (affs[s] - t) * inv, -1.0)))

    def refine(encs):
        def by_frac(i, cs):
            step = pltpu.bitcast(jnp.zeros((N_EXPERTS, 1), I32) + jnp.left_shift(126 - i, 23), F32)
            out = []
            for s in range(nseg):
                c_try = cs[s] + step
                out.append(jnp.where(count(encs[s] >= c_try) >= cap, c_try, cs[s]))
            return tuple(out)

        cs = lax.fori_loop(0, REFINE_BITS, by_frac, col(F32))
        w = 2.0 ** -REFINE_BITS
        return [jnp.where(e >= c + w, 2.0, jnp.where(e >= c, (e - c) * (1.0 / w), -1.0))
                for e, c in zip(encs, cs)]

    encs = refine(refine(encs))
    enc = jnp.concatenate(encs, axis=1) if nseg > 1 else encs[0]
    need = [cap - count(e >= 1.5) for e in encs]

    r = lax.broadcasted_iota(I32, (TB, TB), 0)
    c = lax.broadcasted_iota(I32, (TB, TB), 1)
    tri = jnp.where(r < c, 1.0, 0.0).astype(BF16)
    chunk = lambda v, cb: v[:, cb * TB:(cb + 1) * TB]
    stack = lambda masks: jnp.concatenate(
        [jnp.where(mk, 1.0, 0.0).astype(BF16) for mk in masks], axis=0)
    rows = lambda v, cb: v[cb * N_EXPERTS:(cb + 1) * N_EXPERTS]

    above = [chunk(enc, cb) >= 1.5 for cb in range(n_chunks)]
    tie = [(chunk(enc, cb) >= 0.0) & (chunk(enc, cb) < 1.5) for cb in range(n_chunks)]
    tie_rank = _bdot(stack(tie), tri)
    sel = []
    tie_carry = None
    for cb in range(n_chunks):
        if (cb * TB) % seg == 0:
            tie_carry = jnp.zeros((N_EXPERTS, 1), F32)
        keep = tie[cb] & (rows(tie_rank, cb) + tie_carry < need[(cb * TB) // seg])
        sel.append(above[cb] | keep)
        tie_carry = tie_carry + count(tie[cb])
    pos_local = _bdot(stack(sel), tri)

    lane = lax.broadcasted_iota(I32, (N_EXPERTS, 128), 1)
    lane8 = lax.broadcasted_iota(I32, (8, 128), 1)
    units_per_grp = GRP_BLK // UNIT_BLK
    for u in range(n_units):
        base = (u % units_per_grp) * (GRP_ROWS // units_per_grp)
        sel_carry = jnp.zeros((N_EXPERTS, 1), F32)
        meta_off = jnp.zeros((N_EXPERTS, 128), I32)
        meta_np = jnp.zeros((8, 128), I32)
        for j in range(UNIT_BLK):
            cb = u * UNIT_BLK + j
            pos = (rows(pos_local, cb) + sel_carry).astype(I32) + base
            off16 = (sel_carry.astype(I32) + base) & ~15
            sel_carry = sel_carry + count(sel[cb])
            rel = jnp.where(sel[cb], pos - off16, -1)
            rel_ref[:, cb * TB:(cb + 1) * TB] = rel
            gate_ref[:, cb * TB:(cb + 1) * TB] = jnp.where(sel[cb], chunk(aff, cb), 0.0)
            mx = jnp.max(jnp.max(rel, axis=1, keepdims=True), axis=0, keepdims=True)
            npass = jnp.zeros((1, 1), I32)
            for k in range(MAX_PASSES):
                npass = npass + jnp.where(mx >= k * WIN, 1, 0)
            meta_off = jnp.where(lane == j, off16, meta_off)
            meta_np = jnp.where(lane8 == j, npass, meta_np)
        meta_ref[u, 0:N_EXPERTS, :] = meta_off
        meta_ref[u, N_EXPERTS:N_EXPERTS + 8, :] = meta_np


def _route(logits, seg, name):
    n = logits.shape[1]
    n_units = n // UNIT
    return pl.pallas_call(
        functools.partial(_route_kernel, seg=seg, n_units=n_units),
        out_shape=[jax.ShapeDtypeStruct((N_EXPERTS, n), I32),
                   jax.ShapeDtypeStruct((N_EXPERTS, n), F32),
                   jax.ShapeDtypeStruct((n_units, N_EXPERTS + 8, 128), I32)],
        grid=(1,),
        in_specs=[pl.BlockSpec((N_EXPERTS, n), lambda i: (0, 0))],
        out_specs=[pl.BlockSpec((N_EXPERTS, n), lambda i: (0, 0)),
                   pl.BlockSpec((N_EXPERTS, n), lambda i: (0, 0)),
                   pl.BlockSpec((n_units, N_EXPERTS + 8, 128), lambda i: (0, 0, 0))],
        compiler_params=_cparams(1),
        name=name,
    )(logits)


def _moe_kernel(off_ref, np_ref, h2_ref, rel_ref, gt_ref, wg_ref, wu_ref, wd_ref, x1_ref,
                mods_ref, fg_ref, o_ref, xy_ref, *, final, g0, n_grp):
    g = pl.program_id(0) + g0
    s = pl.program_id(1)
    ctx_grp = N_CTX_BLK // GRP_BLK

    def by_group_kind(body):
        if g0 + n_grp <= ctx_grp:
            body(CTX_WIN, False)
        elif g0 >= ctx_grp:
            body(WIN, True)
        else:
            pl.when(g < ctx_grp)(lambda: body(CTX_WIN, False))
            pl.when(g >= ctx_grp)(lambda: body(WIN, True))

    def window_start(e, blk, p, win):
        return pl.multiple_of(off_ref[(g * N_EXPERTS + e) * GRP_BLK + blk] + p * win, 16)

    def passes(blk):
        return np_ref[g * GRP_BLK + blk]

    def one_hot(rel, p, gates, win):
        rows = lax.broadcasted_iota(I32, (win, TB), 0) + p * win
        parts = []
        for e in range(N_EXPERTS):
            hit = jnp.broadcast_to(rel[e:e + 1, :], (win, TB)) == rows
            val = 1.0 if gates is None else jnp.broadcast_to(gates[e:e + 1, :], (win, TB))
            parts.append(jnp.where(hit, val, 0.0).astype(BF16))
        return jnp.concatenate(parts, axis=0)

    @pl.when(s == 0)
    def _zero():
        xy_ref[...] = jnp.zeros(xy_ref.shape, BF16)

    def gather(win, multi_pass):
        for j in range(MOE_BPS):
            blk = s * MOE_BPS + j
            rel = rel_ref[:, j * TB:(j + 1) * TB]
            hb = h2_ref[j * TB:(j + 1) * TB, :]

            def one_pass(p):
                got = _bdot(one_hot(rel, p, None, win), hb).astype(BF16)
                for e in range(N_EXPERTS):
                    rows = pl.ds(window_start(e, blk, p, win), win)
                    xy_ref[e, rows, :] = xy_ref[e, rows, :] + got[e * win:(e + 1) * win]

            one_pass(0)
            if multi_pass:
                def extra(p, carry):
                    one_pass(p)
                    return carry

                lax.fori_loop(1, passes(blk), extra, 0)

    pl.when(s < MOE_TS)(lambda: by_group_kind(gather))

    @pl.when((s >= MOE_TS) & (s < MOE_TS + N_EXPERTS))
    def _ffn():
        e = s - MOE_TS
        x = xy_ref[e, pl.ds(0, GRP_ROWS), :]
        a = _bdot(x, wg_ref[0, 0].astype(BF16))
        hid = (_silu(a) * _bdot(x, wu_ref[0, 0].astype(BF16))).astype(BF16)
        xy_ref[e, pl.ds(0, GRP_ROWS), :] = _bdot(hid, wd_ref[0, 0].astype(BF16)).astype(BF16)

    def combine(win, multi_pass):
        gate2 = mods_ref[0][:, 5 * D:6 * D]
        for j in range(MOE_BPS):
            blk = (s - (MOE_TS + N_EXPERTS)) * MOE_BPS + j
            rel = rel_ref[:, j * TB:(j + 1) * TB]
            gates = gt_ref[:, j * TB:(j + 1) * TB]

            def one_pass(p):
                wins = [xy_ref[e, pl.ds(window_start(e, blk, p, win), win), :]
                        for e in range(N_EXPERTS)]
                return lax.dot_general(one_hot(rel, p, gates, win), jnp.concatenate(wins, axis=0),
                                       TN, preferred_element_type=F32)

            y = one_pass(0)
            if multi_pass:
                y = lax.fori_loop(1, passes(blk), lambda p, acc: acc + one_pass(p), y)
            out = x1_ref[j * TB:(j + 1) * TB, :] + gate2 * y
            if final:
                out = _rms_rows(out) * fg_ref[...]
            o_ref[j * TB:(j + 1) * TB, :] = out

    pl.when(s >= MOE_TS + N_EXPERTS)(lambda: by_group_kind(combine))


def _moe_layer(layer, off16, npass, h2, rel, gates, wg, wu, wd, x1, mods, final_g, final,
               g0=0, n_grp=N_GRP):
    n_steps = 2 * MOE_TS + N_EXPERTS
    tbs = MOE_BPS * TB

    def gather_blk(g, s, *_):
        return (g0 + g) * MOE_TS + jnp.minimum(s, MOE_TS - 1)

    def route_blk(g, s, *_):
        return (g0 + g) * MOE_TS + jnp.where(s < MOE_TS + N_EXPERTS, jnp.minimum(s, MOE_TS - 1),
                                             s - (MOE_TS + N_EXPERTS))

    def combine_blk(g, s, *_):
        return (g0 + g) * MOE_TS + jnp.maximum(s - (MOE_TS + N_EXPERTS), 0)

    def expert(g, s, *_):
        return jnp.clip(s - MOE_TS, 0, N_EXPERTS - 1)

    grid_spec = pltpu.PrefetchScalarGridSpec(
        num_scalar_prefetch=2,
        grid=(n_grp, n_steps),
        in_specs=[
            pl.BlockSpec((tbs, D), lambda *a: (gather_blk(*a), 0)),
            pl.BlockSpec((N_EXPERTS, tbs), lambda *a: (0, route_blk(*a))),
            pl.BlockSpec((N_EXPERTS, tbs), lambda *a: (0, route_blk(*a))),
            pl.BlockSpec((1, 1, D, EXPERT_FF), lambda *a: (layer, expert(*a), 0, 0)),
            pl.BlockSpec((1, 1, D, EXPERT_FF), lambda *a: (layer, expert(*a), 0, 0)),
            pl.BlockSpec((1, 1, EXPERT_FF, D), lambda *a: (layer, expert(*a), 0, 0)),
            pl.BlockSpec((tbs, D), lambda *a: (combine_blk(*a), 0)),
            pl.BlockSpec((1, 1, 6 * D),
                         lambda *a: (layer * N_COND + _cond_row(combine_blk(*a), tbs), 0, 0)),
            pl.BlockSpec((1, D), lambda *a: (0, 0)),
        ],
        out_specs=pl.BlockSpec((tbs, D), lambda *a: (combine_blk(*a) - g0 * MOE_TS, 0)),
        scratch_shapes=[pltpu.VMEM((N_EXPERTS, XY_ROWS, D), BF16)],
    )
    return pl.pallas_call(
        functools.partial(_moe_kernel, final=final, g0=g0, n_grp=n_grp),
        out_shape=jax.ShapeDtypeStruct((n_grp * GRP_BLK * TB, D), F32),
        grid_spec=grid_spec,
        compiler_params=_cparams(2),
        name="moe_experts",
    )(off16, npass, h2, rel, gates, wg, wu, wd, x1, mods, final_g)


def _moe(layer, x1, h2, logits, wg, wu, wd, mods, final_g, final):
    rel_c, gate_c, meta_c = _route(logits[:, :N_CTX_TOK], CTX_LEN, "route_ctx")
    rel_l, gate_l, meta_l = _route(logits[:, N_CTX_TOK:], LAT_LEN, "route_lat")
    rel = jnp.concatenate([rel_c, rel_l], axis=1)
    gates = jnp.concatenate([gate_c, gate_l], axis=1)
    meta = jnp.concatenate([meta_c, meta_l], axis=0)
    upg = GRP_BLK // UNIT_BLK
    off16 = meta[:, :N_EXPERTS, :UNIT_BLK].reshape(N_GRP, upg, N_EXPERTS, UNIT_BLK)
    off16 = jnp.swapaxes(off16, 1, 2).reshape(-1)
    npass = meta[:, N_EXPERTS, :UNIT_BLK].reshape(-1)
    args = (layer, off16, npass, h2, rel, gates, wg, wu, wd, x1, mods, final_g, final)
    if not final:
        return _moe_layer(*args)
    ctx_grp = N_CTX_BLK // GRP_BLK
    return _moe_layer(*args, g0=0, n_grp=ctx_grp), _moe_layer(*args, g0=ctx_grp, n_grp=N_GRP - ctx_grp)


def _axial_tables(rot_dim):
    t = jnp.arange(LAT_LEN)
    row = (t // GRID_W).astype(F32)
    col = (t % GRID_W).astype(F32)
    axis_dim = rot_dim // 2
    inv = ROPE_THETA ** (-jnp.arange(0, axis_dim, 2, dtype=F32) / axis_dim)
    ang = jnp.concatenate([row[:, None] * inv, col[:, None] * inv], axis=-1)
    return jnp.cos(ang), jnp.sin(ang)


def kernel(x_prompt, x_sample, c, cache_mla_ckv, cache_mla_krope, cache_gqa_k, cache_gqa_v, c_ctx, ada_w, ada_b, norm1_g, norm2_g, final_g, pool_w, pool_scale, mla_w_down, mla_g_q, mla_w_uq, mla_g_kv, mla_w_ukv, mla_w_o, conv_w_pw1, conv_b_pw1, conv_w_dw, conv_b_dw, conv_ln_g, conv_ln_b, conv_w_pw2, conv_b_pw2, gqa_w_qkv, gqa_g_q, gqa_g_k, gqa_w_o, router_w, moe_w_gate, moe_w_up, moe_w_down):
    cond =jnp.zeros((N_COND, D), F32).at[0].set(c_ctx).at[1:1 + N_LAT_SEQ].set(c)
    mods = _adaln(cond, ada_w, ada_b).reshape(DEPTH * N_COND, 1, 6 * D)

    row = lambda v: v.reshape(1, -1)
    rw_t = jnp.swapaxes(router_w, 1, 2)
    rwh = rw_t.astype(BF16)
    rw2 = jnp.concatenate([rwh, (rw_t - rwh.astype(F32)).astype(BF16)], axis=1)
    wg, wu, wd = moe_w_gate, moe_w_up, moe_w_down
    fg = row(final_g)

    def post_args(l):
        return row(norm2_g[l]), rw2[l]

    def moe(l, x1, h2, lg):
        return _moe(l, x1, h2, lg, wg, wu, wd, mods, fg, l == DEPTH - 1)

    x1, h2, lg = _pool_layer(x_prompt.reshape(N_CTX_TOK, D), x_sample.reshape(N_LAT_TOK, D), mods,
                             row(norm1_g[0]), pool_w[0].astype(BF16), row(pool_scale[0]),
                             *post_args(0))
    x = moe(0, x1, h2, lg)

    cos, sin = _axial_tables(MLA_ROPE)
    z = jnp.zeros((LAT_LEN, 64), F32)
    cos_t = jnp.concatenate([cos, cos, z], axis=1)
    sin_t = jnp.concatenate([-sin, sin, z], axis=1)
    w_down = jnp.pad(mla_w_down[0], ((0, 0), (0, MLA_DOWN_PAD - mla_w_down.shape[2]))).astype(BF16)
    w_uq = jnp.pad(mla_w_uq[0].reshape(MLA_Q_LORA, MLA_HEADS, MLA_NOPE + MLA_ROPE),
                   ((0, 0), (0, 0), (0, MLA_QK - MLA_NOPE - MLA_ROPE)))
    w_uq = w_uq.reshape(MLA_Q_LORA, MLA_HEADS * MLA_QK).astype(BF16)
    w_ukv = mla_w_ukv[0].astype(BF16)
    qc, ql, kmat, vmat, ckv_n, krp = _mla_proj(
        x, mods, row(norm1_g[1]), w_down, row(mla_g_q[0]), w_uq, row(mla_g_kv[0]), w_ukv, cos_t, sin_t)
    cache_kr = jnp.pad(cache_mla_krope[:, 0].reshape(N_LAT_SEQ * PAST_LEN, MLA_ROPE),
                       ((0, 0), (0, 128 - MLA_ROPE)))
    kc, vc = _mla_cache(cache_mla_ckv[:, 0].reshape(N_LAT_SEQ * PAST_LEN, MLA_KV_LORA), cache_kr, w_ukv)
    a_ctx = _attn_ctx(qc, kmat, vmat, MLA_HEADS, MLA_HEADS, MLA_QK, "mla_attn_ctx")
    a_lat = _attn_lat(qc, ql, kc, kmat, vc, vmat, MLA_HEADS, MLA_HEADS, MLA_QK, "mla_attn_lat")
    x1, h2, lg = _oproj_layer(1, a_ctx, a_lat, mla_w_o[0].astype(BF16), x, mods, *post_args(1))
    x = moe(1, x1, h2, lg)
    state_ckv = ckv_n.reshape(N_CTX_SEQ, 1, CTX_LEN, MLA_KV_LORA)
    state_kr = krp.reshape(N_CTX_SEQ, 1, CTX_LEN, MLA_ROPE)

    x1, h2, lg = _conv_layer(x, mods, row(norm1_g[2]), conv_w_pw1[0].astype(BF16), row(conv_b_pw1[0]),
                             conv_w_dw[0], row(conv_b_dw[0]), row(conv_ln_g[0]), row(conv_ln_b[0]),
                             conv_w_pw2[0].astype(BF16), row(conv_b_pw2[0]), *post_args(2))
    x = moe(2, x1, h2, lg)

    cos, sin = _axial_tables(GQA_HEAD_DIM)
    cos_t = jnp.concatenate([cos, cos], axis=1)
    sin_t = jnp.concatenate([-sin, sin], axis=1)
    qc, ql, kk, vv, kst, vst = _gqa_proj(x, mods, row(norm1_g[3]), gqa_w_qkv[0].astype(BF16),
                                         row(gqa_g_q[0]), row(gqa_g_k[0]), cos_t, sin_t)
    kvw = GQA_KV_HEADS * GQA_HEAD_DIM
    kc = cache_gqa_k[:, 0].reshape(N_LAT_SEQ * PAST_LEN, kvw).astype(BF16)
    vc = cache_gqa_v[:, 0].reshape(N_LAT_SEQ * PAST_LEN, kvw).astype(BF16)
    a_ctx = _attn_ctx(qc, kk, vv, GQA_HEADS, GQA_KV_HEADS, GQA_HEAD_DIM, "gqa_attn_ctx")
    a_lat = _attn_lat(qc, ql, kc, kk, vc, vv, GQA_HEADS, GQA_KV_HEADS, GQA_HEAD_DIM, "gqa_attn_lat")
    x1, h2, lg = _oproj_layer(3, a_ctx, a_lat, gqa_w_o[0].astype(BF16), x, mods, *post_args(3))
    y_ctx, y_lat = moe(3, x1, h2, lg)
    state_k = kst.reshape(N_CTX_SEQ, 1, CTX_LEN, GQA_KV_HEADS, GQA_HEAD_DIM)
    state_v = vst.reshape(N_CTX_SEQ, 1, CTX_LEN, GQA_KV_HEADS, GQA_HEAD_DIM)

    y_prompt = y_ctx.reshape(N_CTX_SEQ, CTX_LEN, D)
    y_sample = y_lat.reshape(N_LAT_SEQ, LAT_LEN, D)
    return (y_prompt, y_sample, state_ckv, state_kr, state_k, state_v)
```

```python
import functools
import math

import jax
import jax.numpy as jnp
from jax import lax
from jax.experimental import pallas as pl
from jax.experimental.pallas import tpu as pltpu

F32 = jnp.float32
BF16 = jnp.bfloat16
I32 = jnp.int32

D = 1024
DEPTH = 4
EPS = 1e-6
GRID_W = 64
ROPE_THETA = 10000.0
LOG2E = math.log2(math.e)

TB = 256
TBW = 512
TBP = 256
HALO = 16
N_CTX_SEQ, CTX_LEN = 32, 256
N_LAT_SEQ, LAT_LEN = 8, 2048
PAST_LEN = 512
N_CTX_TOK = N_CTX_SEQ * CTX_LEN
N_LAT_TOK = N_LAT_SEQ * LAT_LEN
N_TOK = N_CTX_TOK + N_LAT_TOK
N_CTX_BLK = N_CTX_TOK // TB
LAT_BLK = LAT_LEN // TB
N_BLK = N_TOK // TB
N_COND = 16

POOL_WINDOWS = (2, 4, 8, 16)
POOL_GW = 256
MLA_HEADS = 8
MLA_Q_LORA = 384
MLA_KV_LORA = 256
MLA_NOPE = 128
MLA_ROPE = 64
MLA_DOWN_PAD = 768
MLA_QK = 256
MLA_SCALE = (MLA_NOPE + MLA_ROPE) ** -0.5
CONV_WIDTH = 31
GQA_HEADS = 8
GQA_KV_HEADS = 2
GQA_HEAD_DIM = 128
GQA_SCALE = GQA_HEAD_DIM ** -0.5
N_EXPERTS = 16
EXPERT_FF = 512

UNIT = 2048
UNIT_BLK = UNIT // TB
GRP_BLK = 16
N_GRP = N_BLK // GRP_BLK
GRP_ROWS = 512
WIN = 64
MAX_PASSES = -(-(TB + 15) // WIN)
CTX_WIN = CTX_LEN // 8
XY_ROWS = GRP_ROWS + WIN
MOE_BPS = 2
MOE_TS = GRP_BLK // MOE_BPS
MOE_EPS = 2
MOE_FS = N_EXPERTS // MOE_EPS

NT = (((1,), (1,)), ((), ()))
TN = (((0,), (0,)), ((), ()))

VMEM_LIMIT = 56 * 1024 * 1024


def _cparams(n_axes):
    return pltpu.CompilerParams(dimension_semantics=("arbitrary",) * n_axes,
                                vmem_limit_bytes=VMEM_LIMIT)


def _cond_row(b, tb=TB):
    n_ctx = N_CTX_TOK // tb
    return jnp.where(b < n_ctx, 0, 1 + (b - n_ctx) // (LAT_LEN // tb))


def _lat_blk(b, tb=TB):
    n_ctx = N_CTX_TOK // tb
    return jnp.where(b < n_ctx, 0, (b - n_ctx) % (LAT_LEN // tb))


def _mods_spec(layer, blk_of=lambda *g: g[0], tb=TB):
    return pl.BlockSpec((1, 1, 6 * D),
                        lambda *g: (layer * N_COND + _cond_row(blk_of(*g), tb), 0, 0))


def _full(shape):
    return pl.BlockSpec(shape, lambda *g: (0,) * len(shape))


def _tok(width, blk_of=lambda *g: g[0], tb=TB):
    return pl.BlockSpec((tb, width), lambda *g: (blk_of(*g), 0))


def _ctx_only(width, tb=TB):
    return pl.BlockSpec((tb, width), lambda b: (jnp.minimum(b, N_CTX_TOK // tb - 1), 0))


def _lat_only(width, tb=TB):
    return pl.BlockSpec((tb, width), lambda b: (jnp.maximum(b - N_CTX_TOK // tb, 0), 0))


def _halo_specs(width):
    per = TB // HALO
    prev = pl.BlockSpec((HALO, width), lambda b: (jnp.maximum(b * per - 1, 0), 0))
    nxt = pl.BlockSpec((HALO, width), lambda b: (jnp.minimum((b + 1) * per, N_TOK // HALO - 1), 0))
    return prev, nxt


def _seq_edges(b):
    lat = b >= N_CTX_BLK
    j = _lat_blk(b)
    return lat & (j > 0), lat & (j < LAT_BLK - 1)


def _rms_rows(x):
    return x * lax.rsqrt(jnp.mean(x * x, axis=-1, keepdims=True) + EPS)


def _modulate(x, g, shift, scale):
    return _rms_rows(x) * g * (1.0 + scale) + shift


def _silu(x):
    return x * jax.nn.sigmoid(x)


def _bdot(a, b):
    return jnp.dot(a, b, preferred_element_type=F32)


def _adaln_kernel(c_ref, w_ref, b_ref, o_ref):
    s = _silu(c_ref[...]).astype(BF16)
    o_ref[0] = _bdot(s, w_ref[0].astype(BF16)) + b_ref[0]


def _adaln(cond, ada_w, ada_b):
    tn = 1536
    return pl.pallas_call(
        _adaln_kernel,
        out_shape=jax.ShapeDtypeStruct((DEPTH, N_COND, 6 * D), F32),
        grid=(DEPTH, 6 * D // tn),
        in_specs=[
            _full((N_COND, D)),
            pl.BlockSpec((1, D, tn), lambda l, n: (l, 0, n)),
            pl.BlockSpec((1, 1, tn), lambda l, n: (l, 0, n)),
        ],
        out_specs=pl.BlockSpec((1, N_COND, tn), lambda l, n: (l, 0, n)),
        compiler_params=_cparams(2),
        name="adaln",
    )(cond, ada_w, ada_b.reshape(DEPTH, 1, 6 * D))


def _post(x, y, m, n2g_ref, rw_ref, x1_ref, h2_ref, lg_ref, rows=slice(None)):
    x1 = x + m[:, 2 * D:3 * D] * y
    x1_ref[rows, :] = x1
    h2 = _modulate(x1, n2g_ref[...], m[:, 3 * D:4 * D], m[:, 4 * D:5 * D])
    hh = h2.astype(BF16)
    hl = (h2 - hh.astype(F32)).astype(BF16)
    h2_ref[rows, :] = hh
    n = x.shape[0]
    r = lax.dot_general(rw_ref[...], jnp.concatenate([hh, hl], axis=0), NT,
                        preferred_element_type=F32)
    lg_ref[:, rows] = r[:N_EXPERTS, :n] + r[N_EXPERTS:, :n] + r[:N_EXPERTS, n:]


_POST_IN = [_full((1, D)), _full((2 * N_EXPERTS, D))]
_POST_OUT_SHAPES = [jax.ShapeDtypeStruct((N_TOK, D), F32),
                    jax.ShapeDtypeStruct((N_TOK, D), BF16),
                    jax.ShapeDtypeStruct((N_EXPERTS, N_TOK), F32)]


def _post_out_specs(tb=TB):
    return [_tok(D, tb=tb), _tok(D, tb=tb), pl.BlockSpec((N_EXPERTS, tb), lambda b: (0, b))]


def _pool_kernel(xc_ref, xl_ref, xp_ref, xn_ref, mods_ref, n1g_ref, pw_ref, ps_ref,
                 n2g_ref, rw_ref, x1_ref, h2_ref, lg_ref):
    b = pl.program_id(0)
    lat = b >= N_CTX_TOK // TBW
    j = _lat_blk(b, TBW)
    m = mods_ref[0]
    g, sh, sc = n1g_ref[...], m[:, 0:D], m[:, D:2 * D]
    x = jnp.where(lat, xl_ref[...], xc_ref[...])
    h = _modulate(x, g, sh, sc)
    hext = jnp.concatenate([_modulate(xp_ref[...], g, sh, sc), h,
                            _modulate(xn_ref[...], g, sh, sc)], axis=0)
    ext = TB + 2 * HALO
    n_sub = TBW // TB
    t = lax.broadcasted_iota(I32, (TB, ext), 0)
    u = lax.broadcasted_iota(I32, (TB, ext), 1) - HALO
    tc = lax.broadcasted_iota(I32, (TB, 1), 0)
    for i in range(n_sub):
        has_prev = lat & (j > 0) if i == 0 else lat
        has_next = lat & (j < LAT_LEN // TBW - 1) if i == n_sub - 1 else lat
        valid = ((u >= 0) | has_prev) & ((u < TB) | has_next)
        lo_lim = jnp.where(has_prev, -HALO, 0)
        hi_lim = jnp.where(has_next, TB + HALO, TB)
        he = hext[i * TB:i * TB + ext]
        hc = h[i * TB:(i + 1) * TB]
        ys = []
        for gi, win in enumerate(POOL_WINDOWS):
            hw = win // 2
            band = jnp.where((u >= t - hw) & (u < t + hw) & valid, 1.0, 0.0).astype(BF16)
            cnt = (jnp.minimum(tc + hw, hi_lim) - jnp.maximum(tc - hw, lo_lim)).astype(F32)
            hg = he[:, gi * POOL_GW:(gi + 1) * POOL_GW]
            hi = hg.astype(BF16)
            lo = (hg - hi.astype(F32)).astype(BF16)
            tot = _bdot(band, hi) + _bdot(band, lo)
            pooled = tot / cnt - hc[:, gi * POOL_GW:(gi + 1) * POOL_GW]
            ys.append(_bdot(pooled.astype(BF16), pw_ref[gi]))
        y = jnp.concatenate(ys, axis=1) * ps_ref[...]
        _post(x[i * TB:(i + 1) * TB], y, m, n2g_ref, rw_ref, x1_ref, h2_ref, lg_ref,
              pl.ds(i * TB, TB))


def _pool_layer(x_ctx, x_lat, mods, n1g, pool_w, pool_scale, n2g, rw):
    per = TBW // HALO
    lat_b = lambda b: jnp.maximum(b - N_CTX_TOK // TBW, 0)
    prev = pl.BlockSpec((HALO, D), lambda b: (jnp.maximum(lat_b(b) * per - 1, 0), 0))
    nxt = pl.BlockSpec((HALO, D),
                       lambda b: (jnp.minimum((lat_b(b) + 1) * per, N_LAT_TOK // HALO - 1), 0))
    return pl.pallas_call(
        _pool_kernel,
        out_shape=_POST_OUT_SHAPES,
        grid=(N_TOK // TBW,),
        in_specs=[_ctx_only(D, TBW), _lat_only(D, TBW), prev, nxt, _mods_spec(0, tb=TBW),
                  _full((1, D)), _full((4, POOL_GW, POOL_GW)), _full((1, D))] + _POST_IN,
        out_specs=_post_out_specs(TBW),
        compiler_params=_cparams(1),
        name="pool_mixer",
    )(x_ctx, x_lat, x_lat, x_lat, mods, n1g, pool_w, pool_scale, n2g, rw)


def _rope_half32(v, cos_t, sin_t):
    lane = lax.broadcasted_iota(I32, v.shape, 1)
    swapped = jnp.where(lane < 32, pltpu.roll(v, 96, axis=1), pltpu.roll(v, 32, axis=1))
    return v * cos_t + swapped * sin_t


def _mla_proj_kernel(x_ref, mods_ref, n1g_ref, wd_ref, gq_ref, wuq_ref, gkv_ref, wukv_ref,
                     cos_ref, sin_ref, qc_ref, ql_ref, k_ref, v_ref, ckv_ref, kr_ref):
    lat = pl.program_id(0) >= N_CTX_TOK // TBP
    m = mods_ref[0]
    h = _modulate(x_ref[...], n1g_ref[...], m[:, 0:D], m[:, D:2 * D])
    d = _bdot(h.astype(BF16), wd_ref[...])
    cq = d[:, :MLA_Q_LORA]
    ckv = d[:, MLA_Q_LORA:MLA_Q_LORA + MLA_KV_LORA]
    krp = d[:, MLA_Q_LORA + MLA_KV_LORA:]
    q = _bdot((_rms_rows(cq) * gq_ref[...]).astype(BF16), wuq_ref[...])
    ckv_n = _rms_rows(ckv) * gkv_ref[...]
    kv = _bdot(ckv_n.astype(BF16), wukv_ref[...])
    cos_t, sin_t = cos_ref[...], sin_ref[...]
    k_rope = jnp.where(lat, _rope_half32(krp, cos_t, sin_t), krp).astype(BF16)
    qs = MLA_SCALE * LOG2E
    for hd in range(MLA_HEADS):
        c0 = hd * MLA_QK
        qn = (q[:, c0:c0 + 128] * qs).astype(BF16)
        qr = q[:, c0 + 128:c0 + 256] * qs
        qc_ref[:, c0:c0 + 128] = qn
        qc_ref[:, c0 + 128:c0 + 256] = qr.astype(BF16)
        ql_ref[:, c0:c0 + 128] = qn
        ql_ref[:, c0 + 128:c0 + 256] = _rope_half32(qr, cos_t, sin_t).astype(BF16)
        k_ref[:, c0:c0 + 128] = kv[:, c0:c0 + 128].astype(BF16)
        k_ref[:, c0 + 128:c0 + 256] = k_rope
        v_ref[:, hd * 128:(hd + 1) * 128] = kv[:, c0 + 128:c0 + 256].astype(BF16)

    @pl.when(jnp.logical_not(lat))
    def _state():
        ckv_ref[...] = ckv_n
        kr_ref[...] = krp[:, :MLA_ROPE]


def _mla_proj(x, mods, n1g, wd, gq, wuq, gkv, wukv, cos_t, sin_t):
    hq = MLA_HEADS * MLA_QK
    tw = functools.partial(_tok, tb=TBP)
    rope_spec = pl.BlockSpec((TBP, 128), lambda b: (_lat_blk(b, TBP), 0))
    return pl.pallas_call(
        _mla_proj_kernel,
        out_shape=[jax.ShapeDtypeStruct((N_TOK, hq), BF16),
                   jax.ShapeDtypeStruct((N_LAT_TOK, hq), BF16),
                   jax.ShapeDtypeStruct((N_TOK, hq), BF16),
                   jax.ShapeDtypeStruct((N_TOK, MLA_HEADS * 128), BF16),
                   jax.ShapeDtypeStruct((N_CTX_TOK, MLA_KV_LORA), F32),
                   jax.ShapeDtypeStruct((N_CTX_TOK, MLA_ROPE), F32)],
        grid=(N_TOK // TBP,),
        in_specs=[tw(D), _mods_spec(1, tb=TBP), _full((1, D)), _full((D, MLA_DOWN_PAD)),
                  _full((1, MLA_Q_LORA)), _full((MLA_Q_LORA, hq)), _full((1, MLA_KV_LORA)),
                  _full((MLA_KV_LORA, hq)), rope_spec, rope_spec],
        out_specs=[tw(hq), _lat_only(hq, TBP), tw(hq), tw(MLA_HEADS * 128),
                   _ctx_only(MLA_KV_LORA, TBP), _ctx_only(MLA_ROPE, TBP)],
        compiler_params=_cparams(1),
        name="mla_proj",
    )(x, mods, n1g, wd, gq, wuq, gkv, wukv, cos_t, sin_t)


def _mla_cache_kernel(ckv_ref, kr_ref, wukv_ref, k_ref, v_ref):
    kv = _bdot(ckv_ref[...].astype(BF16), wukv_ref[...])
    kr = kr_ref[...].astype(BF16)
    for hd in range(MLA_HEADS):
        c0 = hd * MLA_QK
        k_ref[:, c0:c0 + 128] = kv[:, c0:c0 + 128].astype(BF16)
        k_ref[:, c0 + 128:c0 + 256] = kr
        v_ref[:, hd * 128:(hd + 1) * 128] = kv[:, c0 + 128:c0 + 256].astype(BF16)


def _mla_cache(ckv, krp, wukv):
    n = ckv.shape[0]
    hq = MLA_HEADS * MLA_QK
    return pl.pallas_call(
        _mla_cache_kernel,
        out_shape=[jax.ShapeDtypeStruct((n, hq), BF16),
                   jax.ShapeDtypeStruct((n, MLA_HEADS * 128), BF16)],
        grid=(n // TB,),
        in_specs=[_tok(MLA_KV_LORA), _tok(128), _full((MLA_KV_LORA, hq))],
        out_specs=[_tok(hq), _tok(MLA_HEADS * 128)],
        compiler_params=_cparams(1),
        name="mla_cache_kv",
    )(ckv, krp, wukv)


def _attn_ctx_kernel(q_ref, k_ref, v_ref, o_ref, *, heads, kv_heads, dk):
    rep = heads // kv_heads
    for hd in range(heads):
        kh = hd // rep
        q = q_ref[:, hd * dk:(hd + 1) * dk]
        k = k_ref[:, kh * dk:(kh + 1) * dk]
        s = lax.dot_general(q, k, NT, preferred_element_type=F32)
        p = jnp.exp2(s - jnp.max(s, axis=-1, keepdims=True))
        l = jnp.sum(p, axis=-1, keepdims=True)
        o = _bdot(p.astype(BF16), v_ref[:, kh * 128:(kh + 1) * 128])
        o_ref[:, hd * 128:(hd + 1) * 128] = (o / l).astype(BF16)


def _attn_ctx(q, k, v, heads, kv_heads, dk, name):
    return pl.pallas_call(
        functools.partial(_attn_ctx_kernel, heads=heads, kv_heads=kv_heads, dk=dk),
        out_shape=jax.ShapeDtypeStruct((N_CTX_TOK, heads * 128), BF16),
        grid=(N_CTX_SEQ,),
        in_specs=[_tok(heads * dk), _tok(kv_heads * dk), _tok(kv_heads * 128)],
        out_specs=_tok(heads * 128),
        compiler_params=_cparams(1),
        name=name,
    )(q, k, v)


ATT_TQ = 2048


ATT_SUB = 256


def _attn_lat_kernel(qc_ref, ql_ref, kc_ref, kl_ref, vc_ref, vl_ref, o_ref):
    for i in range(ATT_TQ // ATT_SUB):
        rows = pl.ds(i * ATT_SUB, ATT_SUB)
        sc = lax.dot_general(qc_ref[rows, :], kc_ref[...], NT, preferred_element_type=F32)
        sl = lax.dot_general(ql_ref[rows, :], kl_ref[...], NT, preferred_element_type=F32)
        mx = jnp.maximum(jnp.max(sc, axis=-1, keepdims=True), jnp.max(sl, axis=-1, keepdims=True))
        pc = jnp.exp2(sc - mx)
        pn = jnp.exp2(sl - mx)
        l = jnp.sum(pc, axis=-1, keepdims=True) + jnp.sum(pn, axis=-1, keepdims=True)
        o = _bdot(pc.astype(BF16), vc_ref[...]) + _bdot(pn.astype(BF16), vl_ref[...])
        o_ref[rows, :] = (o / l).astype(BF16)


def _attn_lat(qc, ql, kc, kl, vc, vl, heads, kv_heads, dk, name):
    rep = heads // kv_heads
    nq = LAT_LEN // ATT_TQ
    ctx_q_blk = N_CTX_TOK // ATT_TQ
    ctx_kv_blk = N_CTX_TOK // LAT_LEN
    return pl.pallas_call(
        _attn_lat_kernel,
        out_shape=jax.ShapeDtypeStruct((N_LAT_TOK, heads * 128), BF16),
        grid=(N_LAT_SEQ, heads, nq),
        in_specs=[
            pl.BlockSpec((ATT_TQ, dk), lambda b, h, i: (ctx_q_blk + b * nq + i, h)),
            pl.BlockSpec((ATT_TQ, dk), lambda b, h, i: (b * nq + i, h)),
            pl.BlockSpec((PAST_LEN, dk), lambda b, h, i: (b, h // rep)),
            pl.BlockSpec((LAT_LEN, dk), lambda b, h, i: (ctx_kv_blk + b, h // rep)),
            pl.BlockSpec((PAST_LEN, 128), lambda b, h, i: (b, h // rep)),
            pl.BlockSpec((LAT_LEN, 128), lambda b, h, i: (ctx_kv_blk + b, h // rep)),
        ],
        out_specs=pl.BlockSpec((ATT_TQ, 128), lambda b, h, i: (b * nq + i, h)),
        compiler_params=_cparams(3),
        name=name,
    )(qc, ql, kc, kl, vc, vl)


def _oproj_kernel(ac_ref, al_ref, w_ref, x_ref, mods_ref,
                  n2g_ref, rw_ref, x1_ref, h2_ref, lg_ref):
    lat = pl.program_id(0) >= N_CTX_TOK // TBW
    a = jnp.where(lat, al_ref[...], ac_ref[...])
    y = _bdot(a, w_ref[...])
    _post(x_ref[...], y, mods_ref[0], n2g_ref, rw_ref, x1_ref, h2_ref, lg_ref)


def _oproj_layer(layer, a_ctx, a_lat, w_o, x, mods, n2g, rw):
    return pl.pallas_call(
        _oproj_kernel,
        out_shape=_POST_OUT_SHAPES,
        grid=(N_TOK // TBW,),
        in_specs=[_ctx_only(D, TBW), _lat_only(D, TBW), _full((D, D)), _tok(D, tb=TBW),
                  _mods_spec(layer, tb=TBW)] + _POST_IN,
        out_specs=_post_out_specs(TBW),
        compiler_params=_cparams(1),
        name="attn_out_proj",
    )(a_ctx, a_lat, w_o, x, mods, n2g, rw)


def _conv_kernel(x_ref, xp_ref, xn_ref, mods_ref, n1g_ref, w1_ref, b1_ref, wdw_ref, bdw_ref,
                 lng_ref, lnb_ref, w2_ref, b2_ref, n2g_ref, rw_ref, x1_ref, h2_ref, lg_ref):
    b = pl.program_id(0)
    has_prev, has_next = _seq_edges(b)
    m = mods_ref[0]
    x = x_ref[...]
    xcat = jnp.concatenate([xp_ref[...], x, xn_ref[...]], axis=0)
    h = _modulate(xcat, n1g_ref[...], m[:, 0:D], m[:, D:2 * D]).astype(BF16)
    ext = TB + 2 * HALO
    row = lax.broadcasted_iota(I32, (ext, 1), 0)
    inside = ((row >= HALO) | has_prev) & ((row < HALO + TB) | has_next)
    wdw = wdw_ref[...]
    half = CONV_WIDTH // 2
    cols = []
    for p0 in range(0, D, 256):
        val = _bdot(h, w1_ref[:, p0:p0 + 256]) + b1_ref[:, p0:p0 + 256]
        gate = _bdot(h, w1_ref[:, D + p0:D + p0 + 256]) + b1_ref[:, D + p0:D + p0 + 256]
        ucat = jnp.where(inside, val * jax.nn.sigmoid(gate), 0.0)
        shifted = [ucat if r == 0 else pltpu.roll(ucat, ext - r, axis=0) for r in range(8)]
        for c0 in range(0, 256, 128):
            acc = jnp.zeros((TB, 128), F32)
            for k in range(CONV_WIDTH):
                a, r = divmod(k + HALO - half, 8)
                acc = acc + (shifted[r][8 * a:8 * a + TB, c0:c0 + 128]
                             * wdw[k:k + 1, p0 + c0:p0 + c0 + 128])
            cols.append(acc)
    u = jnp.concatenate(cols, axis=1) + bdw_ref[...]
    mu = jnp.mean(u, axis=-1, keepdims=True)
    uc = u - mu
    var = jnp.mean(uc * uc, axis=-1, keepdims=True)
    un_ = uc * lax.rsqrt(var + EPS) * lng_ref[...] + lnb_ref[...]
    y = _bdot(_silu(un_).astype(BF16), w2_ref[...]) + b2_ref[...]
    _post(x, y, m, n2g_ref, rw_ref, x1_ref, h2_ref, lg_ref)


def _conv_layer(x, mods, n1g, w_pw1, b_pw1, w_dw, b_dw, ln_g, ln_b, w_pw2, b_pw2, n2g, rw):
    prev, nxt = _halo_specs(D)
    return pl.pallas_call(
        _conv_kernel,
        out_shape=_POST_OUT_SHAPES,
        grid=(N_BLK,),
        in_specs=[_tok(D), prev, nxt, _mods_spec(2), _full((1, D)), _full((D, 2 * D)),
                  _full((1, 2 * D)), _full((CONV_WIDTH, D)), _full((1, D)), _full((1, D)),
                  _full((1, D)), _full((D, D)), _full((1, D))] + _POST_IN,
        out_specs=_post_out_specs(),
        compiler_params=_cparams(1),
        name="conv_module",
    )(x, x, x, mods, n1g, w_pw1, b_pw1, w_dw, b_dw, ln_g, ln_b, w_pw2, b_pw2, n2g, rw)


def _gqa_proj_kernel(x_ref, mods_ref, n1g_ref, w_ref, gq_ref, gk_ref, cos_ref, sin_ref,
                     qc_ref, ql_ref, kk_ref, vv_ref, kst_ref, vst_ref):
    lat = pl.program_id(0) >= N_CTX_TOK // TBP
    m = mods_ref[0]
    h = _modulate(x_ref[...], n1g_ref[...], m[:, 0:D], m[:, D:2 * D])
    qkv = _bdot(h.astype(BF16), w_ref[...])
    cos_t, sin_t = cos_ref[...], sin_ref[...]
    rope = lambda v: v * cos_t + pltpu.roll(v, 64, axis=1) * sin_t
    qs = GQA_SCALE * LOG2E
    for hd in range(GQA_HEADS):
        c0 = hd * 128
        qn = _rms_rows(qkv[:, c0:c0 + 128]) * gq_ref[...] * qs
        qc_ref[:, c0:c0 + 128] = qn.astype(BF16)
        ql_ref[:, c0:c0 + 128] = rope(qn).astype(BF16)
    kns = [_rms_rows(qkv[:, D + kh * 128:D + (kh + 1) * 128]) * gk_ref[...]
           for kh in range(GQA_KV_HEADS)]
    for kh in range(GQA_KV_HEADS):
        kk_ref[:, kh * 128:(kh + 1) * 128] = jnp.where(lat, rope(kns[kh]), kns[kh]).astype(BF16)
    v = qkv[:, D + 256:D + 512]
    vv_ref[...] = v.astype(BF16)

    @pl.when(jnp.logical_not(lat))
    def _state():
        for kh in range(GQA_KV_HEADS):
            kst_ref[:, kh * 128:(kh + 1) * 128] = kns[kh]
        vst_ref[...] = v


def _gqa_proj(x, mods, n1g, w_qkv, gq, gk, cos_t, sin_t):
    kvw = GQA_KV_HEADS * 128
    tw = functools.partial(_tok, tb=TBP)
    rope_spec = pl.BlockSpec((TBP, 128), lambda b: (_lat_blk(b, TBP), 0))
    return pl.pallas_call(
        _gqa_proj_kernel,
        out_shape=[jax.ShapeDtypeStruct((N_TOK, D), BF16),
                   jax.ShapeDtypeStruct((N_LAT_TOK, D), BF16),
                   jax.ShapeDtypeStruct((N_TOK, kvw), BF16),
                   jax.ShapeDtypeStruct((N_TOK, kvw), BF16),
                   jax.ShapeDtypeStruct((N_CTX_TOK, kvw), F32),
                   jax.ShapeDtypeStruct((N_CTX_TOK, kvw), F32)],
        grid=(N_TOK // TBP,),
        in_specs=[tw(D), _mods_spec(3, tb=TBP), _full((1, D)), _full((D, D + 2 * kvw)),
                  _full((1, 128)), _full((1, 128)), rope_spec, rope_spec],
        out_specs=[tw(D), _lat_only(D, TBP), tw(kvw), tw(kvw),
                   _ctx_only(kvw, TBP), _ctx_only(kvw, TBP)],
        compiler_params=_cparams(1),
        name="gqa_proj",
    )(x, mods, n1g, w_qkv, gq, gk, cos_t, sin_t)


REFINE_BITS = 15
F32_MIN_NORMAL = 2.0 ** -126


def _route_kernel(lg_ref, rel_ref, gate_ref, meta_ref, *, seg, n_units):
    n_tok = n_units * UNIT
    nseg = n_tok // seg
    n_chunks = n_tok // TB
    cap = seg // 8
    lg = lg_ref[...]
    ex = jnp.exp(lg - jnp.max(lg, axis=0, keepdims=True))
    aff = ex / jnp.sum(ex, axis=0, keepdims=True)
    affs = [aff[:, s * seg:(s + 1) * seg] for s in range(nseg)]
    col = lambda dtype: tuple(jnp.zeros((N_EXPERTS, 1), dtype) for _ in range(nseg))

    def count(mask):
        return jnp.sum(jnp.where(mask, 1.0, 0.0), axis=1, keepdims=True)

    def by_bits(i, bits):
        bit = jnp.left_shift(jnp.int32(1), 30 - i)
        out = []
        for s in range(nseg):
            cand = bits[s] | bit
            ok = count(affs[s] >= pltpu.bitcast(cand, F32)) >= cap
            out.append(jnp.where(ok, cand, bits[s]))
        return tuple(out)

    bits = lax.fori_loop(0, 31, by_bits, col(I32))
    encs = []
    for s in range(nseg):
        t = pltpu.bitcast(bits[s], F32)
        tn = pltpu.bitcast(bits[s] + 1, F32)
        inv = 1.0 / jnp.maximum(tn - t, F32_MIN_NORMAL)
        encs.append(jnp.where(affs[s] >= tn, 2.0, jnp.where(affs[s] >= t, (affs[s] - t) * inv, -1.0)))

    def refine(encs):
        def by_frac(i, cs):
            step = pltpu.bitcast(jnp.zeros((N_EXPERTS, 1), I32) + jnp.left_shift(126 - i, 23), F32)
            out = []
            for s in range(nseg):
                c_try = cs[s] + step
                out.append(jnp.where(count(encs[s] >= c_try) >= cap, c_try, cs[s]))
            return tuple(out)

        cs = lax.fori_loop(0, REFINE_BITS, by_frac, col(F32))
        w = 2.0 ** -REFINE_BITS
        return [jnp.where(e >= c + w, 2.0, jnp.where(e >= c, (e - c) * (1.0 / w), -1.0))
                for e, c in zip(encs, cs)]

    encs = refine(refine(encs))
    enc = jnp.concatenate(encs, axis=1) if nseg > 1 else encs[0]
    need = [cap - count(e >= 1.5) for e in encs]

    r = lax.broadcasted_iota(I32, (TB, TB), 0)
    c = lax.broadcasted_iota(I32, (TB, TB), 1)
    tri = jnp.where(r < c, 1.0, 0.0).astype(BF16)
    chunk = lambda v, cb: v[:, cb * TB:(cb + 1) * TB]
    stack = lambda masks: jnp.concatenate(
        [jnp.where(mk, 1.0, 0.0).astype(BF16) for mk in masks], axis=0)
    rows = lambda v, cb: v[cb * N_EXPERTS:(cb + 1) * N_EXPERTS]

    above = [chunk(enc, cb) >= 1.5 for cb in range(n_chunks)]
    tie = [(chunk(enc, cb) >= 0.0) & (chunk(enc, cb) < 1.5) for cb in range(n_chunks)]
    tie_rank = _bdot(stack(tie), tri)
    sel = []
    tie_carry = None
    for cb in range(n_chunks):
        if (cb * TB) % seg == 0:
            tie_carry = jnp.zeros((N_EXPERTS, 1), F32)
        keep = tie[cb] & (rows(tie_rank, cb) + tie_carry < need[(cb * TB) // seg])
        sel.append(above[cb] | keep)
        tie_carry = tie_carry + count(tie[cb])
    pos_local = _bdot(stack(sel), tri)

    lane = lax.broadcasted_iota(I32, (N_EXPERTS, 128), 1)
    lane8 = lax.broadcasted_iota(I32, (8, 128), 1)
    units_per_grp = GRP_BLK // UNIT_BLK
    for u in range(n_units):
        base = (u % units_per_grp) * (GRP_ROWS // units_per_grp)
        sel_carry = jnp.zeros((N_EXPERTS, 1), F32)
        meta_off = jnp.zeros((N_EXPERTS, 128), I32)
        meta_np = jnp.zeros((8, 128), I32)
        for j in range(UNIT_BLK):
            cb = u * UNIT_BLK + j
            pos = (rows(pos_local, cb) + sel_carry).astype(I32) + base
            off16 = (sel_carry.astype(I32) + base) & ~15
            sel_carry = sel_carry + count(sel[cb])
            rel = jnp.where(sel[cb], pos - off16, -1)
            rel_ref[:, cb * TB:(cb + 1) * TB] = rel
            gate_ref[:, cb * TB:(cb + 1) * TB] = jnp.where(sel[cb], chunk(aff, cb), 0.0)
            mx = jnp.max(jnp.max(rel, axis=1, keepdims=True), axis=0, keepdims=True)
            npass = jnp.zeros((1, 1), I32)
            for k in range(MAX_PASSES):
                npass = npass + jnp.where(mx >= k * WIN, 1, 0)
            meta_off = jnp.where(lane == j, off16, meta_off)
            meta_np = jnp.where(lane8 == j, npass, meta_np)
        meta_ref[u, 0:N_EXPERTS, :] = meta_off
        meta_ref[u, N_EXPERTS:N_EXPERTS + 8, :] = meta_np


def _route(logits, seg, name):
    n = logits.shape[1]
    n_units = n // UNIT
    return pl.pallas_call(
        functools.partial(_route_kernel, seg=seg, n_units=n_units),
        out_shape=[jax.ShapeDtypeStruct((N_EXPERTS, n), I32),
                   jax.ShapeDtypeStruct((N_EXPERTS, n), F32),
                   jax.ShapeDtypeStruct((n_units, N_EXPERTS + 8, 128), I32)],
        grid=(1,),
        in_specs=[pl.BlockSpec((N_EXPERTS, n), lambda i: (0, 0))],
        out_specs=[pl.BlockSpec((N_EXPERTS, n), lambda i: (0, 0)),
                   pl.BlockSpec((N_EXPERTS, n), lambda i: (0, 0)),
                   pl.BlockSpec((n_units, N_EXPERTS + 8, 128), lambda i: (0, 0, 0))],
        compiler_params=_cparams(1),
        name=name,
    )(logits)


def _moe_kernel(off_ref, np_ref, h2_ref, rel_ref, gt_ref, wg_ref, wu_ref, wd_ref, x1_ref,
                mods_ref, fg_ref, o_ref, xy_ref, *, final, g0, n_grp):
    g = pl.program_id(0) + g0
    s = pl.program_id(1)
    ctx_grp = N_CTX_BLK // GRP_BLK

    def by_group_kind(body):
        if g0 + n_grp <= ctx_grp:
            body(CTX_WIN, False)
        elif g0 >= ctx_grp:
            body(WIN, True)
        else:
            pl.when(g < ctx_grp)(lambda: body(CTX_WIN, False))
            pl.when(g >= ctx_grp)(lambda: body(WIN, True))

    def window_start(e, blk, p, win):
        return pl.multiple_of(off_ref[(g * N_EXPERTS + e) * GRP_BLK + blk] + p * win, 16)

    def passes(blk):
        return np_ref[g * GRP_BLK + blk]

    def one_hot(rel, p, gates, win):
        rows = lax.broadcasted_iota(I32, (win, TB), 0) + p * win
        parts = []
        for e in range(N_EXPERTS):
            hit = jnp.broadcast_to(rel[e:e + 1, :], (win, TB)) == rows
            val = 1.0 if gates is None else jnp.broadcast_to(gates[e:e + 1, :], (win, TB))
            parts.append(jnp.where(hit, val, 0.0).astype(BF16))
        return jnp.concatenate(parts, axis=0)

    @pl.when(s == 0)
    def _zero():
        xy_ref[...] = jnp.zeros(xy_ref.shape, BF16)

    def gather(win, multi_pass):
        for j in range(MOE_BPS):
            blk = s * MOE_BPS + j
            rel = rel_ref[:, j * TB:(j + 1) * TB]
            hb = h2_ref[j * TB:(j + 1) * TB, :]

            def one_pass(p):
                got = _bdot(one_hot(rel, p, None, win), hb).astype(BF16)
                for e in range(N_EXPERTS):
                    rows = pl.ds(window_start(e, blk, p, win), win)
                    xy_ref[e, rows, :] = xy_ref[e, rows, :] + got[e * win:(e + 1) * win]

            one_pass(0)
            if multi_pass:
                def extra(p, carry):
                    one_pass(p)
                    return carry

                lax.fori_loop(1, passes(blk), extra, 0)

    pl.when(s < MOE_TS)(lambda: by_group_kind(gather))

    @pl.when((s >= MOE_TS) & (s < MOE_TS + MOE_FS))
    def _ffn():
        for k in range(MOE_EPS):
            e = (s - MOE_TS) * MOE_EPS + k
            x = xy_ref[e, pl.ds(0, GRP_ROWS), :]
            a = _bdot(x, wg_ref[0, k])
            hid = (_silu(a) * _bdot(x, wu_ref[0, k])).astype(BF16)
            xy_ref[e, pl.ds(0, GRP_ROWS), :] = _bdot(hid, wd_ref[0, k].astype(BF16)).astype(BF16)

    def combine(win, multi_pass):
        gate2 = mods_ref[0][:, 5 * D:6 * D]
        for j in range(MOE_BPS):
            blk = (s - (MOE_TS + MOE_FS)) * MOE_BPS + j
            rel = rel_ref[:, j * TB:(j + 1) * TB]
            gates = gt_ref[:, j * TB:(j + 1) * TB]

            def one_pass(p):
                wins = [xy_ref[e, pl.ds(window_start(e, blk, p, win), win), :]
                        for e in range(N_EXPERTS)]
                return lax.dot_general(one_hot(rel, p, gates, win), jnp.concatenate(wins, axis=0),
                                       TN, preferred_element_type=F32)

            y = one_pass(0)
            if multi_pass:
                y = lax.fori_loop(1, passes(blk), lambda p, acc: acc + one_pass(p), y)
            out = x1_ref[j * TB:(j + 1) * TB, :] + gate2 * y
            if final:
                out = _rms_rows(out) * fg_ref[...]
            o_ref[j * TB:(j + 1) * TB, :] = out

    pl.when(s >= MOE_TS + MOE_FS)(lambda: by_group_kind(combine))


def _moe_layer(layer, off16, npass, h2, rel, gates, wg, wu, wd, x1, mods, final_g, final,
               g0=0, n_grp=N_GRP):
    n_steps = 2 * MOE_TS + MOE_FS
    tbs = MOE_BPS * TB

    def gather_blk(g, s, *_):
        return (g0 + g) * MOE_TS + jnp.minimum(s, MOE_TS - 1)

    def route_blk(g, s, *_):
        return (g0 + g) * MOE_TS + jnp.where(s < MOE_TS + MOE_FS, jnp.minimum(s, MOE_TS - 1),
                                             s - (MOE_TS + MOE_FS))

    def combine_blk(g, s, *_):
        return (g0 + g) * MOE_TS + jnp.maximum(s - (MOE_TS + MOE_FS), 0)

    def expert(g, s, *_):
        return jnp.clip(s - MOE_TS, 0, MOE_FS - 1)

    grid_spec = pltpu.PrefetchScalarGridSpec(
        num_scalar_prefetch=2,
        grid=(n_grp, n_steps),
        in_specs=[
            pl.BlockSpec((tbs, D), lambda *a: (gather_blk(*a), 0)),
            pl.BlockSpec((N_EXPERTS, tbs), lambda *a: (0, route_blk(*a))),
            pl.BlockSpec((N_EXPERTS, tbs), lambda *a: (0, route_blk(*a))),
            pl.BlockSpec((1, MOE_EPS, D, EXPERT_FF), lambda *a: (layer, expert(*a), 0, 0)),
            pl.BlockSpec((1, MOE_EPS, D, EXPERT_FF), lambda *a: (layer, expert(*a), 0, 0)),
            pl.BlockSpec((1, MOE_EPS, EXPERT_FF, D), lambda *a: (layer, expert(*a), 0, 0)),
            pl.BlockSpec((tbs, D), lambda *a: (combine_blk(*a), 0)),
            pl.BlockSpec((1, 1, 6 * D),
                         lambda *a: (layer * N_COND + _cond_row(combine_blk(*a), tbs), 0, 0)),
            pl.BlockSpec((1, D), lambda *a: (0, 0)),
        ],
        out_specs=pl.BlockSpec((tbs, D), lambda *a: (combine_blk(*a) - g0 * MOE_TS, 0)),
        scratch_shapes=[pltpu.VMEM((N_EXPERTS, XY_ROWS, D), BF16)],
    )
    return pl.pallas_call(
        functools.partial(_moe_kernel, final=final, g0=g0, n_grp=n_grp),
        out_shape=jax.ShapeDtypeStruct((n_grp * GRP_BLK * TB, D), F32),
        grid_spec=grid_spec,
        compiler_params=_cparams(2),
        name="moe_experts",
    )(off16, npass, h2, rel, gates, wg, wu, wd, x1, mods, final_g)


def _moe(layer, x1, h2, logits, wg, wu, wd, mods, final_g, final):
    rel_c, gate_c, meta_c = _route(logits[:, :N_CTX_TOK], CTX_LEN, "route_ctx")
    rel_l, gate_l, meta_l = _route(logits[:, N_CTX_TOK:], LAT_LEN, "route_lat")
    rel = jnp.concatenate([rel_c, rel_l], axis=1)
    gates = jnp.concatenate([gate_c, gate_l], axis=1)
    meta = jnp.concatenate([meta_c, meta_l], axis=0)
    upg = GRP_BLK // UNIT_BLK
    off16 = meta[:, :N_EXPERTS, :UNIT_BLK].reshape(N_GRP, upg, N_EXPERTS, UNIT_BLK)
    off16 = jnp.swapaxes(off16, 1, 2).reshape(-1)
    npass = meta[:, N_EXPERTS, :UNIT_BLK].reshape(-1)
    args = (layer, off16, npass, h2, rel, gates, wg, wu, wd, x1, mods, final_g, final)
    if not final:
        return _moe_layer(*args)
    ctx_grp = N_CTX_BLK // GRP_BLK
    return _moe_layer(*args, g0=0, n_grp=ctx_grp), _moe_layer(*args, g0=ctx_grp, n_grp=N_GRP - ctx_grp)


def _axial_tables(rot_dim):
    t = jnp.arange(LAT_LEN)
    row = (t // GRID_W).astype(F32)
    col = (t % GRID_W).astype(F32)
    axis_dim = rot_dim // 2
    inv = ROPE_THETA ** (-jnp.arange(0, axis_dim, 2, dtype=F32) / axis_dim)
    ang = jnp.concatenate([row[:, None] * inv, col[:, None] * inv], axis=-1)
    return jnp.cos(ang), jnp.sin(ang)


def kernel(x_prompt, x_sample, c, cache_mla_ckv, cache_mla_krope, cache_gqa_k, cache_gqa_v, c_ctx, ada_w, ada_b, norm1_g, norm2_g, final_g, pool_w, pool_scale, mla_w_down, mla_g_q, mla_w_uq, mla_g_kv, mla_w_ukv, mla_w_o, conv_w_pw1, conv_b_pw1, conv_w_dw, conv_b_dw, conv_ln_g, conv_ln_b, conv_w_pw2, conv_b_pw2, gqa_w_qkv, gqa_g_q, gqa_g_k, gqa_w_o, router_w, moe_w_gate, moe_w_up, moe_w_down):
    cond =jnp.zeros((N_COND, D), F32).at[0].set(c_ctx).at[1:1 + N_LAT_SEQ].set(c)
    mods = _adaln(cond, ada_w, ada_b).reshape(DEPTH * N_COND, 1, 6 * D)

    row = lambda v: v.reshape(1, -1)
    rw_t = jnp.swapaxes(router_w, 1, 2)
    rwh = rw_t.astype(BF16)
    rw2 = jnp.concatenate([rwh, (rw_t - rwh.astype(F32)).astype(BF16)], axis=1)
    wg, wu, wd = moe_w_gate.astype(BF16), moe_w_up.astype(BF16), moe_w_down
    fg = row(final_g)

    def post_args(l):
        return row(norm2_g[l]), rw2[l]

    def moe(l, x1, h2, lg):
        return _moe(l, x1, h2, lg, wg, wu, wd, mods, fg, l == DEPTH - 1)

    x1, h2, lg = _pool_layer(x_prompt.reshape(N_CTX_TOK, D), x_sample.reshape(N_LAT_TOK, D), mods,
                             row(norm1_g[0]), pool_w[0].astype(BF16), row(pool_scale[0]),
                             *post_args(0))
    x = moe(0, x1, h2, lg)

    cos, sin = _axial_tables(MLA_ROPE)
    z = jnp.zeros((LAT_LEN, 64), F32)
    cos_t = jnp.concatenate([cos, cos, z], axis=1)
    sin_t = jnp.concatenate([-sin, sin, z], axis=1)
    w_down = jnp.pad(mla_w_down[0], ((0, 0), (0, MLA_DOWN_PAD - mla_w_down.shape[2]))).astype(BF16)
    w_uq = jnp.pad(mla_w_uq[0].reshape(MLA_Q_LORA, MLA_HEADS, MLA_NOPE + MLA_ROPE),
                   ((0, 0), (0, 0), (0, MLA_QK - MLA_NOPE - MLA_ROPE)))
    w_uq = w_uq.reshape(MLA_Q_LORA, MLA_HEADS * MLA_QK).astype(BF16)
    w_ukv = mla_w_ukv[0].astype(BF16)
    qc, ql, kmat, vmat, ckv_n, krp = _mla_proj(
        x, mods, row(norm1_g[1]), w_down, row(mla_g_q[0]), w_uq, row(mla_g_kv[0]), w_ukv, cos_t, sin_t)
    cache_kr = jnp.pad(cache_mla_krope[:, 0].reshape(N_LAT_SEQ * PAST_LEN, MLA_ROPE),
                       ((0, 0), (0, 128 - MLA_ROPE)))
    kc, vc = _mla_cache(cache_mla_ckv[:, 0].reshape(N_LAT_SEQ * PAST_LEN, MLA_KV_LORA), cache_kr, w_ukv)
    a_ctx = _attn_ctx(qc, kmat, vmat, MLA_HEADS, MLA_HEADS, MLA_QK, "mla_attn_ctx")
    a_lat = _attn_lat(qc, ql, kc, kmat, vc, vmat, MLA_HEADS, MLA_HEADS, MLA_QK, "mla_attn_lat")
    x1, h2, lg = _oproj_layer(1, a_ctx, a_lat, mla_w_o[0].astype(BF16), x, mods, *post_args(1))
    x = moe(1, x1, h2, lg)
    state_ckv = ckv_n.reshape(N_CTX_SEQ, 1, CTX_LEN, MLA_KV_LORA)
    state_kr = krp.reshape(N_CTX_SEQ, 1, CTX_LEN, MLA_ROPE)

    x1, h2, lg = _conv_layer(x, mods, row(norm1_g[2]), conv_w_pw1[0].astype(BF16), row(conv_b_pw1[0]),
                             conv_w_dw[0], row(conv_b_dw[0]), row(conv_ln_g[0]), row(conv_ln_b[0]),
                             conv_w_pw2[0].astype(BF16), row(conv_b_pw2[0]), *post_args(2))
    x = moe(2, x1, h2, lg)

    cos, sin = _axial_tables(GQA_HEAD_DIM)
    cos_t = jnp.concatenate([cos, cos], axis=1)
    sin_t = jnp.concatenate([-sin, sin], axis=1)
    qc, ql, kk, vv, kst, vst = _gqa_proj(x, mods, row(norm1_g[3]), gqa_w_qkv[0].astype(BF16),
                                         row(gqa_g_q[0]), row(gqa_g_k[0]), cos_t, sin_t)
    kvw = GQA_KV_HEADS * GQA_HEAD_DIM
    kc = cache_gqa_k[:, 0].reshape(N_LAT_SEQ * PAST_LEN, kvw).astype(BF16)
    vc = cache_gqa_v[:, 0].reshape(N_LAT_SEQ * PAST_LEN, kvw).astype(BF16)
    a_ctx = _attn_ctx(qc, kk, vv, GQA_HEADS, GQA_KV_HEADS, GQA_HEAD_DIM, "gqa_attn_ctx")
    a_lat = _attn_lat(qc, ql, kc, kk, vc, vv, GQA_HEADS, GQA_KV_HEADS, GQA_HEAD_DIM, "gqa_attn_lat")
    x1, h2, lg = _oproj_layer(3, a_ctx, a_lat, gqa_w_o[0].astype(BF16), x, mods, *post_args(3))
    y_ctx, y_lat = moe(3, x1, h2, lg)
    state_k = kst.reshape(N_CTX_SEQ, 1, CTX_LEN, GQA_KV_HEADS, GQA_HEAD_DIM)
    state_v = vst.reshape(N_CTX_SEQ, 1, CTX_LEN, GQA_KV_HEADS, GQA_HEAD_DIM)

    y_prompt = y_ctx.reshape(N_CTX_SEQ, CTX_LEN, D)
    y_sample = y_lat.reshape(N_LAT_SEQ, LAT_LEN, D)
    return (y_prompt, y_sample, state_ckv, state_kr, state_k, state_v)
```

```python
import functools
import math

import jax
import jax.numpy as jnp
from jax import lax
from jax.experimental import pallas as pl
from jax.experimental.pallas import tpu as pltpu

F32 = jnp.float32
BF16 = jnp.bfloat16
I32 = jnp.int32

D = 1024
DEPTH = 4
EPS = 1e-6
GRID_W = 64
ROPE_THETA = 10000.0
LOG2E = math.log2(math.e)

TB = 256
TBW = 1024
TBP = 256
HALO = 16
N_CTX_SEQ, CTX_LEN = 32, 256
N_LAT_SEQ, LAT_LEN = 8, 2048
PAST_LEN = 512
N_CTX_TOK = N_CTX_SEQ * CTX_LEN
N_LAT_TOK = N_LAT_SEQ * LAT_LEN
N_TOK = N_CTX_TOK + N_LAT_TOK
N_CTX_BLK = N_CTX_TOK // TB
LAT_BLK = LAT_LEN // TB
N_BLK = N_TOK // TB
N_COND = 16

POOL_WINDOWS = (2, 4, 8, 16)
POOL_GW = 256
MLA_HEADS = 8
MLA_Q_LORA = 384
MLA_KV_LORA = 256
MLA_NOPE = 128
MLA_ROPE = 64
MLA_DOWN_PAD = 768
MLA_QK = 256
MLA_SCALE = (MLA_NOPE + MLA_ROPE) ** -0.5
CONV_WIDTH = 31
GQA_HEADS = 8
GQA_KV_HEADS = 2
GQA_HEAD_DIM = 128
GQA_SCALE = GQA_HEAD_DIM ** -0.5
N_EXPERTS = 16
EXPERT_FF = 512

UNIT = 2048
UNIT_BLK = UNIT // TB
GRP_BLK = 16
N_GRP = N_BLK // GRP_BLK
GRP_ROWS = 512
WIN = 64
MAX_PASSES = -(-(TB + 15) // WIN)
CTX_WIN = CTX_LEN // 8
XY_ROWS = GRP_ROWS + WIN
MOE_BPS = 2
MOE_TS = GRP_BLK // MOE_BPS
MOE_EPS = 2
MOE_FS = N_EXPERTS // MOE_EPS

NT = (((1,), (1,)), ((), ()))
TN = (((0,), (0,)), ((), ()))

VMEM_LIMIT = 56 * 1024 * 1024


def _cparams(n_axes):
    return pltpu.CompilerParams(dimension_semantics=("arbitrary",) * n_axes,
                                vmem_limit_bytes=VMEM_LIMIT)


def _cond_row(b, tb=TB):
    n_ctx = N_CTX_TOK // tb
    return jnp.where(b < n_ctx, 0, 1 + (b - n_ctx) // (LAT_LEN // tb))


def _lat_blk(b, tb=TB):
    n_ctx = N_CTX_TOK // tb
    return jnp.where(b < n_ctx, 0, (b - n_ctx) % (LAT_LEN // tb))


def _mods_spec(layer, blk_of=lambda *g: g[0], tb=TB):
    return pl.BlockSpec((1, 1, 6 * D),
                        lambda *g: (layer * N_COND + _cond_row(blk_of(*g), tb), 0, 0))


def _full(shape):
    return pl.BlockSpec(shape, lambda *g: (0,) * len(shape))


def _tok(width, blk_of=lambda *g: g[0], tb=TB):
    return pl.BlockSpec((tb, width), lambda *g: (blk_of(*g), 0))


def _ctx_only(width, tb=TB):
    return pl.BlockSpec((tb, width), lambda b: (jnp.minimum(b, N_CTX_TOK // tb - 1), 0))


def _lat_only(width, tb=TB):
    return pl.BlockSpec((tb, width), lambda b: (jnp.maximum(b - N_CTX_TOK // tb, 0), 0))


def _halo_specs(width):
    per = TB // HALO
    prev = pl.BlockSpec((HALO, width), lambda b: (jnp.maximum(b * per - 1, 0), 0))
    nxt = pl.BlockSpec((HALO, width), lambda b: (jnp.minimum((b + 1) * per, N_TOK // HALO - 1), 0))
    return prev, nxt


def _seq_edges(b):
    lat = b >= N_CTX_BLK
    j = _lat_blk(b)
    return lat & (j > 0), lat & (j < LAT_BLK - 1)


def _rms_rows(x):
    return x * lax.rsqrt(jnp.mean(x * x, axis=-1, keepdims=True) + EPS)


def _modulate(x, g, shift, scale):
    return _rms_rows(x) * g * (1.0 + scale) + shift


def _silu(x):
    return x * jax.nn.sigmoid(x)


def _bdot(a, b):
    return jnp.dot(a, b, preferred_element_type=F32)


def _adaln_kernel(c_ref, w_ref, b_ref, o_ref):
    s = _silu(c_ref[...]).astype(BF16)
    o_ref[0] = _bdot(s, w_ref[0].astype(BF16)) + b_ref[0]


def _adaln(cond, ada_w, ada_b):
    tn = 1536
    return pl.pallas_call(
        _adaln_kernel,
        out_shape=jax.ShapeDtypeStruct((DEPTH, N_COND, 6 * D), F32),
        grid=(DEPTH, 6 * D // tn),
        in_specs=[
            _full((N_COND, D)),
            pl.BlockSpec((1, D, tn), lambda l, n: (l, 0, n)),
            pl.BlockSpec((1, 1, tn), lambda l, n: (l, 0, n)),
        ],
        out_specs=pl.BlockSpec((1, N_COND, tn), lambda l, n: (l, 0, n)),
        compiler_params=_cparams(2),
        name="adaln",
    )(cond, ada_w, ada_b.reshape(DEPTH, 1, 6 * D))


def _post(x, y, m, n2g_ref, rw_ref, x1_ref, h2_ref, lg_ref, rows=slice(None)):
    x1 = x + m[:, 2 * D:3 * D] * y
    x1_ref[rows, :] = x1
    h2 = _modulate(x1, n2g_ref[...], m[:, 3 * D:4 * D], m[:, 4 * D:5 * D])
    hh = h2.astype(BF16)
    hl = (h2 - hh.astype(F32)).astype(BF16)
    h2_ref[rows, :] = hh
    n = x.shape[0]
    r = lax.dot_general(rw_ref[...], jnp.concatenate([hh, hl], axis=0), NT,
                        preferred_element_type=F32)
    lg_ref[:, rows] = r[:N_EXPERTS, :n] + r[N_EXPERTS:, :n] + r[:N_EXPERTS, n:]


_POST_IN = [_full((1, D)), _full((2 * N_EXPERTS, D))]
_POST_OUT_SHAPES = [jax.ShapeDtypeStruct((N_TOK, D), F32),
                    jax.ShapeDtypeStruct((N_TOK, D), BF16),
                    jax.ShapeDtypeStruct((N_EXPERTS, N_TOK), F32)]


def _post_out_specs(tb=TB):
    return [_tok(D, tb=tb), _tok(D, tb=tb), pl.BlockSpec((N_EXPERTS, tb), lambda b: (0, b))]


def _pool_kernel(xc_ref, xl_ref, xp_ref, xn_ref, mods_ref, n1g_ref, pw_ref, ps_ref,
                 n2g_ref, rw_ref, x1_ref, h2_ref, lg_ref):
    b = pl.program_id(0)
    lat = b >= N_CTX_TOK // TBW
    j = _lat_blk(b, TBW)
    m = mods_ref[0]
    g, sh, sc = n1g_ref[...], m[:, 0:D], m[:, D:2 * D]
    x = jnp.where(lat, xl_ref[...], xc_ref[...])
    h = _modulate(x, g, sh, sc)
    hext = jnp.concatenate([_modulate(xp_ref[...], g, sh, sc), h,
                            _modulate(xn_ref[...], g, sh, sc)], axis=0)
    ext = TB + 2 * HALO
    n_sub = TBW // TB
    t = lax.broadcasted_iota(I32, (TB, ext), 0)
    u = lax.broadcasted_iota(I32, (TB, ext), 1) - HALO
    tc = lax.broadcasted_iota(I32, (TB, 1), 0)
    for i in range(n_sub):
        has_prev = lat & (j > 0) if i == 0 else lat
        has_next = lat & (j < LAT_LEN // TBW - 1) if i == n_sub - 1 else lat
        valid = ((u >= 0) | has_prev) & ((u < TB) | has_next)
        lo_lim = jnp.where(has_prev, -HALO, 0)
        hi_lim = jnp.where(has_next, TB + HALO, TB)
        he = hext[i * TB:i * TB + ext]
        hc = h[i * TB:(i + 1) * TB]
        ys = []
        for gi, win in enumerate(POOL_WINDOWS):
            hw = win // 2
            band = jnp.where((u >= t - hw) & (u < t + hw) & valid, 1.0, 0.0).astype(BF16)
            cnt = (jnp.minimum(tc + hw, hi_lim) - jnp.maximum(tc - hw, lo_lim)).astype(F32)
            hg = he[:, gi * POOL_GW:(gi + 1) * POOL_GW]
            hi = hg.astype(BF16)
            lo = (hg - hi.astype(F32)).astype(BF16)
            tot = _bdot(band, hi) + _bdot(band, lo)
            pooled = tot / cnt - hc[:, gi * POOL_GW:(gi + 1) * POOL_GW]
            ys.append(_bdot(pooled.astype(BF16), pw_ref[gi]))
        y = jnp.concatenate(ys, axis=1) * ps_ref[...]
        _post(x[i * TB:(i + 1) * TB], y, m, n2g_ref, rw_ref, x1_ref, h2_ref, lg_ref,
              pl.ds(i * TB, TB))


def _pool_layer(x_ctx, x_lat, mods, n1g, pool_w, pool_scale, n2g, rw):
    per = TBW // HALO
    lat_b = lambda b: jnp.maximum(b - N_CTX_TOK // TBW, 0)
    prev = pl.BlockSpec((HALO, D), lambda b: (jnp.maximum(lat_b(b) * per - 1, 0), 0))
    nxt = pl.BlockSpec((HALO, D),
                       lambda b: (jnp.minimum((lat_b(b) + 1) * per, N_LAT_TOK // HALO - 1), 0))
    return pl.pallas_call(
        _pool_kernel,
        out_shape=_POST_OUT_SHAPES,
        grid=(N_TOK // TBW,),
        in_specs=[_ctx_only(D, TBW), _lat_only(D, TBW), prev, nxt, _mods_spec(0, tb=TBW),
                  _full((1, D)), _full((4, POOL_GW, POOL_GW)), _full((1, D))] + _POST_IN,
        out_specs=_post_out_specs(TBW),
        compiler_params=_cparams(1),
        name="pool_mixer",
    )(x_ctx, x_lat, x_lat, x_lat, mods, n1g, pool_w, pool_scale, n2g, rw)


def _rope_half32(v, cos_t, sin_t):
    lane = lax.broadcasted_iota(I32, v.shape, 1)
    swapped = jnp.where(lane < 32, pltpu.roll(v, 96, axis=1), pltpu.roll(v, 32, axis=1))
    return v * cos_t + swapped * sin_t


def _mla_proj_kernel(x_ref, mods_ref, n1g_ref, wd_ref, gq_ref, wuq_ref, gkv_ref, wukv_ref,
                     cos_ref, sin_ref, qc_ref, ql_ref, k_ref, v_ref, ckv_ref, kr_ref):
    lat = pl.program_id(0) >= N_CTX_TOK // TBP
    m = mods_ref[0]
    h = _modulate(x_ref[...], n1g_ref[...], m[:, 0:D], m[:, D:2 * D])
    d = _bdot(h.astype(BF16), wd_ref[...])
    cq = d[:, :MLA_Q_LORA]
    ckv = d[:, MLA_Q_LORA:MLA_Q_LORA + MLA_KV_LORA]
    krp = d[:, MLA_Q_LORA + MLA_KV_LORA:]
    q = _bdot((_rms_rows(cq) * gq_ref[...]).astype(BF16), wuq_ref[...])
    ckv_n = _rms_rows(ckv) * gkv_ref[...]
    kv = _bdot(ckv_n.astype(BF16), wukv_ref[...])
    cos_t, sin_t = cos_ref[...], sin_ref[...]
    k_rope = jnp.where(lat, _rope_half32(krp, cos_t, sin_t), krp).astype(BF16)
    qs = MLA_SCALE * LOG2E
    for hd in range(MLA_HEADS):
        c0 = hd * MLA_QK
        qn = (q[:, c0:c0 + 128] * qs).astype(BF16)
        qr = q[:, c0 + 128:c0 + 256] * qs
        qc_ref[:, c0:c0 + 128] = qn
        qc_ref[:, c0 + 128:c0 + 256] = qr.astype(BF16)
        ql_ref[:, c0:c0 + 128] = qn
        ql_ref[:, c0 + 128:c0 + 256] = _rope_half32(qr, cos_t, sin_t).astype(BF16)
        k_ref[:, c0:c0 + 128] = kv[:, c0:c0 + 128].astype(BF16)
        k_ref[:, c0 + 128:c0 + 256] = k_rope
        v_ref[:, hd * 128:(hd + 1) * 128] = kv[:, c0 + 128:c0 + 256].astype(BF16)

    @pl.when(jnp.logical_not(lat))
    def _state():
        ckv_ref[...] = ckv_n
        kr_ref[...] = krp[:, :MLA_ROPE]


def _mla_proj(x, mods, n1g, wd, gq, wuq, gkv, wukv, cos_t, sin_t):
    hq = MLA_HEADS * MLA_QK
    tw = functools.partial(_tok, tb=TBP)
    rope_spec = pl.BlockSpec((TBP, 128), lambda b: (_lat_blk(b, TBP), 0))
    return pl.pallas_call(
        _mla_proj_kernel,
        out_shape=[jax.ShapeDtypeStruct((N_TOK, hq), BF16),
                   jax.ShapeDtypeStruct((N_LAT_TOK, hq), BF16),
                   jax.ShapeDtypeStruct((N_TOK, hq), BF16),
                   jax.ShapeDtypeStruct((N_TOK, MLA_HEADS * 128), BF16),
                   jax.ShapeDtypeStruct((N_CTX_TOK, MLA_KV_LORA), F32),
                   jax.ShapeDtypeStruct((N_CTX_TOK, MLA_ROPE), F32)],
        grid=(N_TOK // TBP,),
        in_specs=[tw(D), _mods_spec(1, tb=TBP), _full((1, D)), _full((D, MLA_DOWN_PAD)),
                  _full((1, MLA_Q_LORA)), _full((MLA_Q_LORA, hq)), _full((1, MLA_KV_LORA)),
                  _full((MLA_KV_LORA, hq)), rope_spec, rope_spec],
        out_specs=[tw(hq), _lat_only(hq, TBP), tw(hq), tw(MLA_HEADS * 128),
                   _ctx_only(MLA_KV_LORA, TBP), _ctx_only(MLA_ROPE, TBP)],
        compiler_params=_cparams(1),
        name="mla_proj",
    )(x, mods, n1g, wd, gq, wuq, gkv, wukv, cos_t, sin_t)


def _mla_cache_kernel(ckv_ref, kr_ref, wukv_ref, k_ref, v_ref):
    kv = _bdot(ckv_ref[...].astype(BF16), wukv_ref[...])
    kr = kr_ref[...].astype(BF16)
    for hd in range(MLA_HEADS):
        c0 = hd * MLA_QK
        k_ref[:, c0:c0 + 128] = kv[:, c0:c0 + 128].astype(BF16)
        k_ref[:, c0 + 128:c0 + 256] = kr
        v_ref[:, hd * 128:(hd + 1) * 128] = kv[:, c0 + 128:c0 + 256].astype(BF16)


def _mla_cache(ckv, krp, wukv):
    n = ckv.shape[0]
    hq = MLA_HEADS * MLA_QK
    return pl.pallas_call(
        _mla_cache_kernel,
        out_shape=[jax.ShapeDtypeStruct((n, hq), BF16),
                   jax.ShapeDtypeStruct((n, MLA_HEADS * 128), BF16)],
        grid=(n // TB,),
        in_specs=[_tok(MLA_KV_LORA), _tok(128), _full((MLA_KV_LORA, hq))],
        out_specs=[_tok(hq), _tok(MLA_HEADS * 128)],
        compiler_params=_cparams(1),
        name="mla_cache_kv",
    )(ckv, krp, wukv)


def _attn_ctx_kernel(q_ref, k_ref, v_ref, o_ref, *, heads, kv_heads, dk):
    rep = heads // kv_heads
    for hd in range(heads):
        kh = hd // rep
        q = q_ref[:, hd * dk:(hd + 1) * dk]
        k = k_ref[:, kh * dk:(kh + 1) * dk]
        s = lax.dot_general(q, k, NT, preferred_element_type=F32)
        p = jnp.exp2(s - jnp.max(s, axis=-1, keepdims=True))
        l = jnp.sum(p, axis=-1, keepdims=True)
        o = _bdot(p.astype(BF16), v_ref[:, kh * 128:(kh + 1) * 128])
        o_ref[:, hd * 128:(hd + 1) * 128] = (o / l).astype(BF16)


def _attn_ctx(q, k, v, heads, kv_heads, dk, name):
    return pl.pallas_call(
        functools.partial(_attn_ctx_kernel, heads=heads, kv_heads=kv_heads, dk=dk),
        out_shape=jax.ShapeDtypeStruct((N_CTX_TOK, heads * 128), BF16),
        grid=(N_CTX_SEQ,),
        in_specs=[_tok(heads * dk), _tok(kv_heads * dk), _tok(kv_heads * 128)],
        out_specs=_tok(heads * 128),
        compiler_params=_cparams(1),
        name=name,
    )(q, k, v)


ATT_TQ = 2048


ATT_SUB = 256


def _attn_lat_kernel(qc_ref, ql_ref, kc_ref, kl_ref, vc_ref, vl_ref, o_ref):
    for i in range(ATT_TQ // ATT_SUB):
        rows = pl.ds(i * ATT_SUB, ATT_SUB)
        sc = lax.dot_general(qc_ref[rows, :], kc_ref[...], NT, preferred_element_type=F32)
        sl = lax.dot_general(ql_ref[rows, :], kl_ref[...], NT, preferred_element_type=F32)
        mx = jnp.maximum(jnp.max(sc, axis=-1, keepdims=True), jnp.max(sl, axis=-1, keepdims=True))
        pc = jnp.exp2(sc - mx)
        pn = jnp.exp2(sl - mx)
        l = jnp.sum(pc, axis=-1, keepdims=True) + jnp.sum(pn, axis=-1, keepdims=True)
        o = _bdot(pc.astype(BF16), vc_ref[...]) + _bdot(pn.astype(BF16), vl_ref[...])
        o_ref[rows, :] = (o / l).astype(BF16)


def _attn_lat(qc, ql, kc, kl, vc, vl, heads, kv_heads, dk, name):
    rep = heads // kv_heads
    nq = LAT_LEN // ATT_TQ
    ctx_q_blk = N_CTX_TOK // ATT_TQ
    ctx_kv_blk = N_CTX_TOK // LAT_LEN
    return pl.pallas_call(
        _attn_lat_kernel,
        out_shape=jax.ShapeDtypeStruct((N_LAT_TOK, heads * 128), BF16),
        grid=(N_LAT_SEQ, heads, nq),
        in_specs=[
            pl.BlockSpec((ATT_TQ, dk), lambda b, h, i: (ctx_q_blk + b * nq + i, h)),
            pl.BlockSpec((ATT_TQ, dk), lambda b, h, i: (b * nq + i, h)),
            pl.BlockSpec((PAST_LEN, dk), lambda b, h, i: (b, h // rep)),
            pl.BlockSpec((LAT_LEN, dk), lambda b, h, i: (ctx_kv_blk + b, h // rep)),
            pl.BlockSpec((PAST_LEN, 128), lambda b, h, i: (b, h // rep)),
            pl.BlockSpec((LAT_LEN, 128), lambda b, h, i: (ctx_kv_blk + b, h // rep)),
        ],
        out_specs=pl.BlockSpec((ATT_TQ, 128), lambda b, h, i: (b * nq + i, h)),
        compiler_params=_cparams(3),
        name=name,
    )(qc, ql, kc, kl, vc, vl)


def _oproj_kernel(ac_ref, al_ref, w_ref, x_ref, mods_ref,
                  n2g_ref, rw_ref, x1_ref, h2_ref, lg_ref):
    lat = pl.program_id(0) >= N_CTX_TOK // TBW
    a = jnp.where(lat, al_ref[...], ac_ref[...])
    y = _bdot(a, w_ref[...])
    _post(x_ref[...], y, mods_ref[0], n2g_ref, rw_ref, x1_ref, h2_ref, lg_ref)


def _oproj_layer(layer, a_ctx, a_lat, w_o, x, mods, n2g, rw):
    return pl.pallas_call(
        _oproj_kernel,
        out_shape=_POST_OUT_SHAPES,
        grid=(N_TOK // TBW,),
        in_specs=[_ctx_only(D, TBW), _lat_only(D, TBW), _full((D, D)), _tok(D, tb=TBW),
                  _mods_spec(layer, tb=TBW)] + _POST_IN,
        out_specs=_post_out_specs(TBW),
        compiler_params=_cparams(1),
        name="attn_out_proj",
    )(a_ctx, a_lat, w_o, x, mods, n2g, rw)


def _conv_kernel(x_ref, xp_ref, xn_ref, mods_ref, n1g_ref, w1_ref, b1_ref, wdw_ref, bdw_ref,
                 lng_ref, lnb_ref, w2_ref, b2_ref, n2g_ref, rw_ref, x1_ref, h2_ref, lg_ref):
    b = pl.program_id(0)
    has_prev, has_next = _seq_edges(b)
    m = mods_ref[0]
    x = x_ref[...]
    xcat = jnp.concatenate([xp_ref[...], x, xn_ref[...]], axis=0)
    h = _modulate(xcat, n1g_ref[...], m[:, 0:D], m[:, D:2 * D]).astype(BF16)
    ext = TB + 2 * HALO
    row = lax.broadcasted_iota(I32, (ext, 1), 0)
    inside = ((row >= HALO) | has_prev) & ((row < HALO + TB) | has_next)
    wdw = wdw_ref[...]
    half = CONV_WIDTH // 2
    cols = []
    for p0 in range(0, D, 256):
        val = _bdot(h, w1_ref[:, p0:p0 + 256]) + b1_ref[:, p0:p0 + 256]
        gate = _bdot(h, w1_ref[:, D + p0:D + p0 + 256]) + b1_ref[:, D + p0:D + p0 + 256]
        ucat = jnp.where(inside, val * jax.nn.sigmoid(gate), 0.0)
        shifted = [ucat if r == 0 else pltpu.roll(ucat, ext - r, axis=0) for r in range(8)]
        for c0 in range(0, 256, 128):
            acc = jnp.zeros((TB, 128), F32)
            for k in range(CONV_WIDTH):
                a, r = divmod(k + HALO - half, 8)
                acc = acc + (shifted[r][8 * a:8 * a + TB, c0:c0 + 128]
                             * wdw[k:k + 1, p0 + c0:p0 + c0 + 128])
            cols.append(acc)
    u = jnp.concatenate(cols, axis=1) + bdw_ref[...]
    mu = jnp.mean(u, axis=-1, keepdims=True)
    uc = u - mu
    var = jnp.mean(uc * uc, axis=-1, keepdims=True)
    un_ = uc * lax.rsqrt(var + EPS) * lng_ref[...] + lnb_ref[...]
    y = _bdot(_silu(un_).astype(BF16), w2_ref[...]) + b2_ref[...]
    _post(x, y, m, n2g_ref, rw_ref, x1_ref, h2_ref, lg_ref)


def _conv_layer(x, mods, n1g, w_pw1, b_pw1, w_dw, b_dw, ln_g, ln_b, w_pw2, b_pw2, n2g, rw):
    prev, nxt = _halo_specs(D)
    return pl.pallas_call(
        _conv_kernel,
        out_shape=_POST_OUT_SHAPES,
        grid=(N_BLK,),
        in_specs=[_tok(D), prev, nxt, _mods_spec(2), _full((1, D)), _full((D, 2 * D)),
                  _full((1, 2 * D)), _full((CONV_WIDTH, D)), _full((1, D)), _full((1, D)),
                  _full((1, D)), _full((D, D)), _full((1, D))] + _POST_IN,
        out_specs=_post_out_specs(),
        compiler_params=_cparams(1),
        name="conv_module",
    )(x, x, x, mods, n1g, w_pw1, b_pw1, w_dw, b_dw, ln_g, ln_b, w_pw2, b_pw2, n2g, rw)


def _gqa_proj_kernel(x_ref, mods_ref, n1g_ref, w_ref, gq_ref, gk_ref, cos_ref, sin_ref,
                     qc_ref, ql_ref, kk_ref, vv_ref, kst_ref, vst_ref):
    lat = pl.program_id(0) >= N_CTX_TOK // TBP
    m = mods_ref[0]
    h = _modulate(x_ref[...], n1g_ref[...], m[:, 0:D], m[:, D:2 * D])
    qkv = _bdot(h.astype(BF16), w_ref[...])
    cos_t, sin_t = cos_ref[...], sin_ref[...]
    rope = lambda v: v * cos_t + pltpu.roll(v, 64, axis=1) * sin_t
    qs = GQA_SCALE * LOG2E
    for hd in range(GQA_HEADS):
        c0 = hd * 128
        qn = _rms_rows(qkv[:, c0:c0 + 128]) * gq_ref[...] * qs
        qc_ref[:, c0:c0 + 128] = qn.astype(BF16)
        ql_ref[:, c0:c0 + 128] = rope(qn).astype(BF16)
    kns = [_rms_rows(qkv[:, D + kh * 128:D + (kh + 1) * 128]) * gk_ref[...]
           for kh in range(GQA_KV_HEADS)]
    for kh in range(GQA_KV_HEADS):
        kk_ref[:, kh * 128:(kh + 1) * 128] = jnp.where(lat, rope(kns[kh]), kns[kh]).astype(BF16)
    v = qkv[:, D + 256:D + 512]
    vv_ref[...] = v.astype(BF16)

    @pl.when(jnp.logical_not(lat))
    def _state():
        for kh in range(GQA_KV_HEADS):
            kst_ref[:, kh * 128:(kh + 1) * 128] = kns[kh]
        vst_ref[...] = v


def _gqa_proj(x, mods, n1g, w_qkv, gq, gk, cos_t, sin_t):
    kvw = GQA_KV_HEADS * 128
    tw = functools.partial(_tok, tb=TBP)
    rope_spec = pl.BlockSpec((TBP, 128), lambda b: (_lat_blk(b, TBP), 0))
    return pl.pallas_call(
        _gqa_proj_kernel,
        out_shape=[jax.ShapeDtypeStruct((N_TOK, D), BF16),
                   jax.ShapeDtypeStruct((N_LAT_TOK, D), BF16),
                   jax.ShapeDtypeStruct((N_TOK, kvw), BF16),
                   jax.ShapeDtypeStruct((N_TOK, kvw), BF16),
                   jax.ShapeDtypeStruct((N_CTX_TOK, kvw), F32),
                   jax.ShapeDtypeStruct((N_CTX_TOK, kvw), F32)],
        grid=(N_TOK // TBP,),
        in_specs=[tw(D), _mods_spec(3, tb=TBP), _full((1, D)), _full((D, D + 2 * kvw)),
                  _full((1, 128)), _full((1, 128)), rope_spec, rope_spec],
        out_specs=[tw(D), _lat_only(D, TBP), tw(kvw), tw(kvw),
                   _ctx_only(kvw, TBP), _ctx_only(kvw, TBP)],
        compiler_params=_cparams(1),
        name="gqa_proj",
    )(x, mods, n1g, w_qkv, gq, gk, cos_t, sin_t)


REFINE_BITS = 15
F32_MIN_NORMAL = 2.0 ** -126


def _route_kernel(lg_ref, rel_ref, gate_ref, meta_ref, *, seg, n_units):
    n_tok = n_units * UNIT
    nseg = n_tok // seg
    n_chunks = n_tok // TB
    cap = seg // 8
    lg = lg_ref[...]
    ex = jnp.exp(lg - jnp.max(lg, axis=0, keepdims=True))
    aff = ex / jnp.sum(ex, axis=0, keepdims=True)
    affs = [aff[:, s * seg:(s + 1) * seg] for s in range(nseg)]
    col = lambda dtype: tuple(jnp.zeros((N_EXPERTS, 1), dtype) for _ in range(nseg))

    def count(mask):
        return jnp.sum(jnp.where(mask, 1.0, 0.0), axis=1, keepdims=True)

    def by_bits(i, bits):
        bit = jnp.left_shift(jnp.int32(1), 30 - i)
        out = []
        for s in range(nseg):
            cand = bits[s] | bit
            ok = count(affs[s] >= pltpu.bitcast(cand, F32)) >= cap
            out.append(jnp.where(ok, cand, bits[s]))
        return tuple(out)

    bits = lax.fori_loop(0, 31, by_bits, col(I32))
    encs = []
    for s in range(nseg):
        t = pltpu.bitcast(bits[s], F32)
        tn = pltpu.bitcast(bits[s] + 1, F32)
        inv = 1.0 / jnp.maximum(tn - t, F32_MIN_NORMAL)
        encs.append(jnp.where(affs[s] >= tn, 2.0, jnp.where(affs[s] >= t, (affs[s] - t) * inv, -1.0)))

    def refine(encs):
        def by_frac(i, cs):
            step = pltpu.bitcast(jnp.zeros((N_EXPERTS, 1), I32) + jnp.left_shift(126 - i, 23), F32)
            out = []
            for s in range(nseg):
                c_try = cs[s] + step
                out.append(jnp.where(count(encs[s] >= c_try) >= cap, c_try, cs[s]))
            return tuple(out)

        cs = lax.fori_loop(0, REFINE_BITS, by_frac, col(F32))
        w = 2.0 ** -REFINE_BITS
        return [jnp.where(e >= c + w, 2.0, jnp.where(e >= c, (e - c) * (1.0 / w), -1.0))
                for e, c in zip(encs, cs)]

    encs = refine(refine(encs))
    enc = jnp.concatenate(encs, axis=1) if nseg > 1 else encs[0]
    need = [cap - count(e >= 1.5) for e in encs]

    r = lax.broadcasted_iota(I32, (TB, TB), 0)
    c = lax.broadcasted_iota(I32, (TB, TB), 1)
    tri = jnp.where(r < c, 1.0, 0.0).astype(BF16)
    chunk = lambda v, cb: v[:, cb * TB:(cb + 1) * TB]
    stack = lambda masks: jnp.concatenate(
        [jnp.where(mk, 1.0, 0.0).astype(BF16) for mk in masks], axis=0)
    rows = lambda v, cb: v[cb * N_EXPERTS:(cb + 1) * N_EXPERTS]

    above = [chunk(enc, cb) >= 1.5 for cb in range(n_chunks)]
    tie = [(chunk(enc, cb) >= 0.0) & (chunk(enc, cb) < 1.5) for cb in range(n_chunks)]
    tie_rank = _bdot(stack(tie), tri)
    sel = []
    tie_carry = None
    for cb in range(n_chunks):
        if (cb * TB) % seg == 0:
            tie_carry = jnp.zeros((N_EXPERTS, 1), F32)
        keep = tie[cb] & (rows(tie_rank, cb) + tie_carry < need[(cb * TB) // seg])
        sel.append(above[cb] | keep)
        tie_carry = tie_carry + count(tie[cb])
    pos_local = _bdot(stack(sel), tri)

    lane = lax.broadcasted_iota(I32, (N_EXPERTS, 128), 1)
    lane8 = lax.broadcasted_iota(I32, (8, 128), 1)
    units_per_grp = GRP_BLK // UNIT_BLK
    for u in range(n_units):
        base = (u % units_per_grp) * (GRP_ROWS // units_per_grp)
        sel_carry = jnp.zeros((N_EXPERTS, 1), F32)
        meta_off = jnp.zeros((N_EXPERTS, 128), I32)
        meta_np = jnp.zeros((8, 128), I32)
        for j in range(UNIT_BLK):
            cb = u * UNIT_BLK + j
            pos = (rows(pos_local, cb) + sel_carry).astype(I32) + base
            off16 = (sel_carry.astype(I32) + base) & ~15
            sel_carry = sel_carry + count(sel[cb])
            rel = jnp.where(sel[cb], pos - off16, -1)
            rel_ref[:, cb * TB:(cb + 1) * TB] = rel
            gate_ref[:, cb * TB:(cb + 1) * TB] = jnp.where(sel[cb], chunk(aff, cb), 0.0)
            mx = jnp.max(jnp.max(rel, axis=1, keepdims=True), axis=0, keepdims=True)
            npass = jnp.zeros((1, 1), I32)
            for k in range(MAX_PASSES):
                npass = npass + jnp.where(mx >= k * WIN, 1, 0)
            meta_off = jnp.where(lane == j, off16, meta_off)
            meta_np = jnp.where(lane8 == j, npass, meta_np)
        meta_ref[u, 0:N_EXPERTS, :] = meta_off
        meta_ref[u, N_EXPERTS:N_EXPERTS + 8, :] = meta_np


def _route(logits, seg, name):
    n = logits.shape[1]
    n_units = n // UNIT
    return pl.pallas_call(
        functools.partial(_route_kernel, seg=seg, n_units=n_units),
        out_shape=[jax.ShapeDtypeStruct((N_EXPERTS, n), I32),
                   jax.ShapeDtypeStruct((N_EXPERTS, n), F32),
                   jax.ShapeDtypeStruct((n_units, N_EXPERTS + 8, 128), I32)],
        grid=(1,),
        in_specs=[pl.BlockSpec((N_EXPERTS, n), lambda i: (0, 0))],
        out_specs=[pl.BlockSpec((N_EXPERTS, n), lambda i: (0, 0)),
                   pl.BlockSpec((N_EXPERTS, n), lambda i: (0, 0)),
                   pl.BlockSpec((n_units, N_EXPERTS + 8, 128), lambda i: (0, 0, 0))],
        compiler_params=_cparams(1),
        name=name,
    )(logits)


def _moe_kernel(off_ref, np_ref, h2_ref, rel_ref, gt_ref, wg_ref, wu_ref, wd_ref, x1_ref,
                mods_ref, fg_ref, o_ref, xy_ref, *, final, g0, n_grp):
    g = pl.program_id(0) + g0
    s = pl.program_id(1)
    ctx_grp = N_CTX_BLK // GRP_BLK

    def by_group_kind(body):
        if g0 + n_grp <= ctx_grp:
            body(CTX_WIN, False)
        elif g0 >= ctx_grp:
            body(WIN, True)
        else:
            pl.when(g < ctx_grp)(lambda: body(CTX_WIN, False))
            pl.when(g >= ctx_grp)(lambda: body(WIN, True))

    def window_start(e, blk, p, win):
        return pl.multiple_of(off_ref[(g * N_EXPERTS + e) * GRP_BLK + blk] + p * win, 16)

    def passes(blk):
        return np_ref[g * GRP_BLK + blk]

    def one_hot(rel, p, gates, win):
        rows = lax.broadcasted_iota(I32, (win, TB), 0) + p * win
        parts = []
        for e in range(N_EXPERTS):
            hit = jnp.broadcast_to(rel[e:e + 1, :], (win, TB)) == rows
            val = 1.0 if gates is None else jnp.broadcast_to(gates[e:e + 1, :], (win, TB))
            parts.append(jnp.where(hit, val, 0.0).astype(BF16))
        return jnp.concatenate(parts, axis=0)

    @pl.when(s == 0)
    def _zero():
        xy_ref[...] = jnp.zeros(xy_ref.shape, BF16)

    def gather(win, multi_pass):
        for j in range(MOE_BPS):
            blk = s * MOE_BPS + j
            rel = rel_ref[:, j * TB:(j + 1) * TB]
            hb = h2_ref[j * TB:(j + 1) * TB, :]

            def one_pass(p):
                got = _bdot(one_hot(rel, p, None, win), hb).astype(BF16)
                for e in range(N_EXPERTS):
                    rows = pl.ds(window_start(e, blk, p, win), win)
                    xy_ref[e, rows, :] = xy_ref[e, rows, :] + got[e * win:(e + 1) * win]

            one_pass(0)
            if multi_pass:
                def extra(p, carry):
                    one_pass(p)
                    return carry

                lax.fori_loop(1, passes(blk), extra, 0)

    pl.when(s < MOE_TS)(lambda: by_group_kind(gather))

    @pl.when((s >= MOE_TS) & (s < MOE_TS + MOE_FS))
    def _ffn():
        for k in range(MOE_EPS):
            e = (s - MOE_TS) * MOE_EPS + k
            x = xy_ref[e, pl.ds(0, GRP_ROWS), :]
            a = _bdot(x, wg_ref[0, k])
            hid = (_silu(a) * _bdot(x, wu_ref[0, k])).astype(BF16)
            xy_ref[e, pl.ds(0, GRP_ROWS), :] = _bdot(hid, wd_ref[0, k].astype(BF16)).astype(BF16)

    def combine(win, multi_pass):
        gate2 = mods_ref[0][:, 5 * D:6 * D]
        for j in range(MOE_BPS):
            blk = (s - (MOE_TS + MOE_FS)) * MOE_BPS + j
            rel = rel_ref[:, j * TB:(j + 1) * TB]
            gates = gt_ref[:, j * TB:(j + 1) * TB]

            def one_pass(p):
                wins = [xy_ref[e, pl.ds(window_start(e, blk, p, win), win), :]
                        for e in range(N_EXPERTS)]
                return lax.dot_general(one_hot(rel, p, gates, win), jnp.concatenate(wins, axis=0),
                                       TN, preferred_element_type=F32)

            y = one_pass(0)
            if multi_pass:
                y = lax.fori_loop(1, passes(blk), lambda p, acc: acc + one_pass(p), y)
            out = x1_ref[j * TB:(j + 1) * TB, :] + gate2 * y
            if final:
                out = _rms_rows(out) * fg_ref[...]
            o_ref[j * TB:(j + 1) * TB, :] = out

    pl.when(s >= MOE_TS + MOE_FS)(lambda: by_group_kind(combine))


def _moe_layer(layer, off16, npass, h2, rel, gates, wg, wu, wd, x1, mods, final_g, final,
               g0=0, n_grp=N_GRP):
    n_steps = 2 * MOE_TS + MOE_FS
    tbs = MOE_BPS * TB

    def gather_blk(g, s, *_):
        return (g0 + g) * MOE_TS + jnp.minimum(s, MOE_TS - 1)

    def route_blk(g, s, *_):
        return (g0 + g) * MOE_TS + jnp.where(s < MOE_TS + MOE_FS, jnp.minimum(s, MOE_TS - 1),
                                             s - (MOE_TS + MOE_FS))

    def combine_blk(g, s, *_):
        return (g0 + g) * MOE_TS + jnp.maximum(s - (MOE_TS + MOE_FS), 0)

    def expert(g, s, *_):
        return jnp.clip(s - MOE_TS, 0, MOE_FS - 1)

    grid_spec = pltpu.PrefetchScalarGridSpec(
        num_scalar_prefetch=2,
        grid=(n_grp, n_steps),
        in_specs=[
            pl.BlockSpec((tbs, D), lambda *a: (gather_blk(*a), 0)),
            pl.BlockSpec((N_EXPERTS, tbs), lambda *a: (0, route_blk(*a))),
            pl.BlockSpec((N_EXPERTS, tbs), lambda *a: (0, route_blk(*a))),
            pl.BlockSpec((1, MOE_EPS, D, EXPERT_FF), lambda *a: (layer, expert(*a), 0, 0)),
            pl.BlockSpec((1, MOE_EPS, D, EXPERT_FF), lambda *a: (layer, expert(*a), 0, 0)),
            pl.BlockSpec((1, MOE_EPS, EXPERT_FF, D), lambda *a: (layer, expert(*a), 0, 0)),
            pl.BlockSpec((tbs, D), lambda *a: (combine_blk(*a), 0)),
            pl.BlockSpec((1, 1, 6 * D),
                         lambda *a: (layer * N_COND + _cond_row(combine_blk(*a), tbs), 0, 0)),
            pl.BlockSpec((1, D), lambda *a: (0, 0)),
        ],
        out_specs=pl.BlockSpec((tbs, D), lambda *a: (combine_blk(*a) - g0 * MOE_TS, 0)),
        scratch_shapes=[pltpu.VMEM((N_EXPERTS, XY_ROWS, D), BF16)],
    )
    return pl.pallas_call(
        functools.partial(_moe_kernel, final=final, g0=g0, n_grp=n_grp),
        out_shape=jax.ShapeDtypeStruct((n_grp * GRP_BLK * TB, D), F32),
        grid_spec=grid_spec,
        compiler_params=_cparams(2),
        name="moe_experts",
    )(off16, npass, h2, rel, gates, wg, wu, wd, x1, mods, final_g)


def _moe(layer, x1, h2, logits, wg, wu, wd, mods, final_g, final):
    rel_c, gate_c, meta_c = _route(logits[:, :N_CTX_TOK], CTX_LEN, "route_ctx")
    rel_l, gate_l, meta_l = _route(logits[:, N_CTX_TOK:], LAT_LEN, "route_lat")
    rel = jnp.concatenate([rel_c, rel_l], axis=1)
    gates = jnp.concatenate([gate_c, gate_l], axis=1)
    meta = jnp.concatenate([meta_c, meta_l], axis=0)
    upg = GRP_BLK // UNIT_BLK
    off16 = meta[:, :N_EXPERTS, :UNIT_BLK].reshape(N_GRP, upg, N_EXPERTS, UNIT_BLK)
    off16 = jnp.swapaxes(off16, 1, 2).reshape(-1)
    npass = meta[:, N_EXPERTS, :UNIT_BLK].reshape(-1)
    args = (layer, off16, npass, h2, rel, gates, wg, wu, wd, x1, mods, final_g, final)
    if not final:
        return _moe_layer(*args)
    ctx_grp = N_CTX_BLK // GRP_BLK
    return _moe_layer(*args, g0=0, n_grp=ctx_grp), _moe_layer(*args, g0=ctx_grp, n_grp=N_GRP - ctx_grp)


def _axial_tables(rot_dim):
    t = jnp.arange(LAT_LEN)
    row = (t // GRID_W).astype(F32)
    col = (t % GRID_W).astype(F32)
    axis_dim = rot_dim // 2
    inv = ROPE_THETA ** (-jnp.arange(0, axis_dim, 2, dtype=F32) / axis_dim)
    ang = jnp.concatenate([row[:, None] * inv, col[:, None] * inv], axis=-1)
    return jnp.cos(ang), jnp.sin(ang)


def kernel(x_prompt, x_sample, c, cache_mla_ckv, cache_mla_krope, cache_gqa_k, cache_gqa_v, c_ctx, ada_w, ada_b, norm1_g, norm2_g, final_g, pool_w, pool_scale, mla_w_down, mla_g_q, mla_w_uq, mla_g_kv, mla_w_ukv, mla_w_o, conv_w_pw1, conv_b_pw1, conv_w_dw, conv_b_dw, conv_ln_g, conv_ln_b, conv_w_pw2, conv_b_pw2, gqa_w_qkv, gqa_g_q, gqa_g_k, gqa_w_o, router_w, moe_w_gate, moe_w_up, moe_w_down):
    cond =jnp.zeros((N_COND, D), F32).at[0].set(c_ctx).at[1:1 + N_LAT_SEQ].set(c)
    mods = _adaln(cond, ada_w, ada_b).reshape(DEPTH * N_COND, 1, 6 * D)

    row = lambda v: v.reshape(1, -1)
    rw_t = jnp.swapaxes(router_w, 1, 2)
    rwh = rw_t.astype(BF16)
    rw2 = jnp.concatenate([rwh, (rw_t - rwh.astype(F32)).astype(BF16)], axis=1)
    wg, wu, wd = moe_w_gate.astype(BF16), moe_w_up.astype(BF16), moe_w_down
    fg = row(final_g)

    def post_args(l):
        return row(norm2_g[l]), rw2[l]

    def moe(l, x1, h2, lg):
        return _moe(l, x1, h2, lg, wg, wu, wd, mods, fg, l == DEPTH - 1)

    x1, h2, lg = _pool_layer(x_prompt.reshape(N_CTX_TOK, D), x_sample.reshape(N_LAT_TOK, D), mods,
                             row(norm1_g[0]), pool_w[0].astype(BF16), row(pool_scale[0]),
                             *post_args(0))
    x = moe(0, x1, h2, lg)

    cos, sin = _axial_tables(MLA_ROPE)
    z = jnp.zeros((LAT_LEN, 64), F32)
    cos_t = jnp.concatenate([cos, cos, z], axis=1)
    sin_t = jnp.concatenate([-sin, sin, z], axis=1)
    w_down = jnp.pad(mla_w_down[0], ((0, 0), (0, MLA_DOWN_PAD - mla_w_down.shape[2]))).astype(BF16)
    w_uq = jnp.pad(mla_w_uq[0].reshape(MLA_Q_LORA, MLA_HEADS, MLA_NOPE + MLA_ROPE),
                   ((0, 0), (0, 0), (0, MLA_QK - MLA_NOPE - MLA_ROPE)))
    w_uq = w_uq.reshape(MLA_Q_LORA, MLA_HEADS * MLA_QK).astype(BF16)
    w_ukv = mla_w_ukv[0].astype(BF16)
    qc, ql, kmat, vmat, ckv_n, krp = _mla_proj(
        x, mods, row(norm1_g[1]), w_down, row(mla_g_q[0]), w_uq, row(mla_g_kv[0]), w_ukv, cos_t, sin_t)
    cache_kr = jnp.pad(cache_mla_krope[:, 0].reshape(N_LAT_SEQ * PAST_LEN, MLA_ROPE),
                       ((0, 0), (0, 128 - MLA_ROPE)))
    kc, vc = _mla_cache(cache_mla_ckv[:, 0].reshape(N_LAT_SEQ * PAST_LEN, MLA_KV_LORA), cache_kr, w_ukv)
    a_ctx = _attn_ctx(qc, kmat, vmat, MLA_HEADS, MLA_HEADS, MLA_QK, "mla_attn_ctx")
    a_lat = _attn_lat(qc, ql, kc, kmat, vc, vmat, MLA_HEADS, MLA_HEADS, MLA_QK, "mla_attn_lat")
    x1, h2, lg = _oproj_layer(1, a_ctx, a_lat, mla_w_o[0].astype(BF16), x, mods, *post_args(1))
    x = moe(1, x1, h2, lg)
    state_ckv = ckv_n.reshape(N_CTX_SEQ, 1, CTX_LEN, MLA_KV_LORA)
    state_kr = krp.reshape(N_CTX_SEQ, 1, CTX_LEN, MLA_ROPE)

    x1, h2, lg = _conv_layer(x, mods, row(norm1_g[2]), conv_w_pw1[0].astype(BF16), row(conv_b_pw1[0]),
                             conv_w_dw[0], row(conv_b_dw[0]), row(conv_ln_g[0]), row(conv_ln_b[0]),
                             conv_w_pw2[0].astype(BF16), row(conv_b_pw2[0]), *post_args(2))
    x = moe(2, x1, h2, lg)

    cos, sin = _axial_tables(GQA_HEAD_DIM)
    cos_t = jnp.concatenate([cos, cos], axis=1)
    sin_t = jnp.concatenate([-sin, sin], axis=1)
    qc, ql, kk, vv, kst, vst = _gqa_proj(x, mods, row(norm1_g[3]), gqa_w_qkv[0].astype(BF16),
                                         row(gqa_g_q[0]), row(gqa_g_k[0]), cos_t, sin_t)
    kvw = GQA_KV_HEADS * GQA_HEAD_DIM
    kc = cache_gqa_k[:, 0].reshape(N_LAT_SEQ * PAST_LEN, kvw).astype(BF16)
    vc = cache_gqa_v[:, 0].reshape(N_LAT_SEQ * PAST_LEN, kvw).astype(BF16)
    a_ctx = _attn_ctx(qc, kk, vv, GQA_HEADS, GQA_KV_HEADS, GQA_HEAD_DIM, "gqa_attn_ctx")
    a_lat = _attn_lat(qc, ql, kc, kk, vc, vv, GQA_HEADS, GQA_KV_HEADS, GQA_HEAD_DIM, "gqa_attn_lat")
    x1, h2, lg = _oproj_layer(3, a_ctx, a_lat, gqa_w_o[0].astype(BF16), x, mods, *post_args(3))
    y_ctx, y_lat = moe(3, x1, h2, lg)
    state_k = kst.reshape(N_CTX_SEQ, 1, CTX_LEN, GQA_KV_HEADS, GQA_HEAD_DIM)
    state_v = vst.reshape(N_CTX_SEQ, 1, CTX_LEN, GQA_KV_HEADS, GQA_HEAD_DIM)

    y_prompt = y_ctx.reshape(N_CTX_SEQ, CTX_LEN, D)
    y_sample = y_lat.reshape(N_LAT_SEQ, LAT_LEN, D)
    return (y_prompt, y_sample, state_ckv, state_kr, state_k, state_v)
```
